```python
import jax
import jax.numpy as jnp
from jax import lax
import numpy as np

D_MODEL = 1024
BATCH = 2
SEQ = 8192
DEPTH = 1
DEC_BATCH = 32
DEC_SEQ = 8
PAST_LEN = 16384
PAGE_SIZE = 128

N_HEADS = 8
HEAD_DIM = 64
ATT_WIDTH = N_HEADS * HEAD_DIM
IDX_HEADS = 8
IDX_DIM = 64
TOPK_MAX = 256
Q_BLOCK = 128
HG_HEADS = 4
HG_DK = 128
HG_DV = 128
HG_QK_WIDTH = HG_HEADS * HG_DK
HG_WIDTH = HG_HEADS * HG_DV
HG_CHUNK = 64
PLE_DIM = 256
RMS_EPS = 1e-6
IN_SPLITS = (ATT_WIDTH, ATT_WIDTH, ATT_WIDTH, ATT_WIDTH, IDX_HEADS * IDX_DIM, IDX_DIM, IDX_HEADS, HG_QK_WIDTH, HG_QK_WIDTH, HG_WIDTH, HG_WIDTH, D_MODEL, D_MODEL)
N_IN = sum(IN_SPLITS)

kernel_name = "hybrid_dsa_hgrn2_gated_merge_step"


def rmsnorm(x, g):
    xf = x.astype(jnp.float32)
    y = xf * lax.rsqrt(jnp.mean(xf * xf, axis=-1, keepdims=True) + RMS_EPS)
    return (y * g.astype(jnp.float32)).astype(x.dtype)


def mixer_inputs(h_n, w_in, lb):
    B, T, _ = h_n.shape
    offsets = [int(o) for o in np.cumsum(IN_SPLITS)[:-1]]
    q, k, v, ga, qi, ki, wi, hq, hf, hv, hg, ma, mb = jnp.split(h_n @ w_in, offsets, axis=-1)
    zf = hf.reshape(B, T, HG_HEADS, HG_DK).astype(jnp.float32)
    lbf = lb.astype(jnp.float32)
    one_minus_lb = 1.0 - lbf
    logf = jnp.log(lbf + one_minus_lb * jax.nn.sigmoid(zf))
    hk = one_minus_lb * jax.nn.sigmoid(-zf)
    return dict(
        q=q.reshape(B, T, N_HEADS, HEAD_DIM), k=k.reshape(B, T, N_HEADS, HEAD_DIM),
        v=v.reshape(B, T, N_HEADS, HEAD_DIM), ga=ga,
        qi=qi.reshape(B, T, IDX_HEADS, IDX_DIM), ki=ki, wi=wi,
        hq=jax.nn.silu(hq.reshape(B, T, HG_HEADS, HG_DK)), hk=hk,
        hv=hv.reshape(B, T, HG_HEADS, HG_DV), logf=logf, hg=hg, ma=ma, mb=mb)


def indexer_scores(qi, wi, ki):
    dots = jnp.einsum('bthd,bsd->bths', qi, ki).astype(jnp.float32) * (IDX_DIM ** -0.5)
    return jnp.einsum('bths,bth->bts', jax.nn.relu(dots), wi.astype(jnp.float32) * (IDX_HEADS ** -0.5))


def sparse_attend(q, k_sel, v_sel, valid):
    s = jnp.einsum('bthd,btkhd->bthk', q, k_sel).astype(jnp.float32) * (HEAD_DIM ** -0.5)
    s = jnp.where(valid[:, :, None, :], s, -jnp.inf)
    p = jax.nn.softmax(s, axis=-1).astype(v_sel.dtype)
    return jnp.einsum('bthk,btkhd->bthd', p, v_sel)


def prompt_sparse_attention(q, k, v, qi, wi, ki):
    B, T = q.shape[:2]
    top = min(TOPK_MAX, T // 4)
    nb = T // Q_BLOCK
    key_pos = jnp.arange(T)
    bidx = jnp.arange(B)[:, None, None]

    def block(args):
        qb, qib, wib, start = args
        qpos = start + jnp.arange(Q_BLOCK)
        sc = indexer_scores(qib, wib, ki)
        sc = jnp.where((key_pos[None, :] <= qpos[:, None])[None], sc, -jnp.inf)
        _, idx = lax.top_k(sc, top)
        valid = idx <= qpos[None, :, None]
        return sparse_attend(qb, k[bidx, idx], v[bidx, idx], valid)

    def to_blocks(a):
        return jnp.moveaxis(a.reshape(B, nb, Q_BLOCK, *a.shape[2:]), 1, 0)

    starts = jnp.arange(nb) * Q_BLOCK
    out = lax.map(block, (to_blocks(q), to_blocks(qi), to_blocks(wi), starts))
    return jnp.moveaxis(out, 0, 1).reshape(B, T, N_HEADS, HEAD_DIM)


def sample_sparse_attention(q, k_new, v_new, qi, wi, ki_new, ck, cv, cki, page_table):
    B, T = q.shape[:2]
    n_pages = page_table.shape[1]
    past = n_pages * PAGE_SIZE
    L = past + T
    top = min(TOPK_MAX, L // 4)
    ki_past = cki[page_table].reshape(B, past, IDX_DIM).astype(ki_new.dtype)
    ki_all = jnp.concatenate([ki_past, ki_new], axis=1)
    qpos = past + jnp.arange(T)
    sc = indexer_scores(qi, wi, ki_all)
    sc = jnp.where((jnp.arange(L)[None, :] <= qpos[:, None])[None], sc, -jnp.inf)
    _, idx = lax.top_k(sc, top)
    bidx = jnp.arange(B)[:, None, None]
    pidx = jnp.minimum(idx, past - 1)
    phys = page_table[bidx, pidx // PAGE_SIZE]
    off = pidx % PAGE_SIZE
    nidx = jnp.clip(idx - past, 0, T - 1)
    in_past = (idx < past)[..., None, None]
    k_sel = jnp.where(in_past, ck[phys, off].astype(k_new.dtype), k_new[bidx, nidx])
    v_sel = jnp.where(in_past, cv[phys, off].astype(v_new.dtype), v_new[bidx, nidx])
    valid = idx <= qpos[None, :, None]
    return sparse_attend(q, k_sel, v_sel, valid)


def hgrn2_chunkwise(q, k, v, logf, s0):
    B, T = q.shape[:2]
    c = HG_CHUNK if T % HG_CHUNK == 0 else T
    n = T // c
    causal = jnp.tril(jnp.ones((c, c), dtype=bool))

    def chunks(a):
        return jnp.moveaxis(a.astype(jnp.float32).reshape(B, n, c, *a.shape[2:]), 1, 0)

    def step(S, inp):
        qc, kc, vc, lc = inp
        cum = jnp.cumsum(lc, axis=1)
        o_inter = jnp.einsum('bthk,bhkv->bthv', qc * jnp.exp(cum), S)
        rel = cum[:, :, None] - cum[:, None, :]
        rel = jnp.where(causal[None, :, :, None, None], rel, -jnp.inf)
        a = jnp.einsum('bthk,bshk,btshk->bhts', qc, kc, jnp.exp(rel))
        o_intra = jnp.einsum('bhts,bshv->bthv', a, vc)
        last = cum[:, -1]
        kd = kc * jnp.exp(last[:, None] - cum)
        S = jnp.exp(last)[..., None] * S + jnp.einsum('bshk,bshv->bhkv', kd, vc)
        return S, o_inter + o_intra

    S, o = lax.scan(step, s0.astype(jnp.float32), (chunks(q), chunks(k), chunks(v), chunks(logf)))
    o = jnp.moveaxis(o, 0, 1).reshape(B, T, HG_HEADS, HG_DV)
    return o, S


def layer_output(h, att, hgo, m, p_i, g_post, norm_g, w_a, w_b, w_o, w_pg, w_pp):
    B, T, _ = h.shape
    ya = (att.reshape(B, T, ATT_WIDTH) * jax.nn.silu(m['ga'])) @ w_a
    hb = rmsnorm(hgo.astype(h.dtype), norm_g).reshape(B, T, HG_WIDTH) * jax.nn.silu(m['hg'])
    yb = hb @ w_b
    mixed = jax.nn.sigmoid(m['ma']) * ya + jax.nn.sigmoid(m['mb']) * yb
    h = h + rmsnorm(mixed @ w_o, g_post)
    return h + jax.nn.sigmoid(h @ w_pg) * (p_i @ w_pp)


def setup_inputs(seed: int = 0) -> dict:
    key = jax.random.key(seed)
    ks = jax.random.split(key, 24)
    n_pages = PAST_LEN // PAGE_SIZE
    n_used = DEC_BATCH * n_pages
    n_phys = n_used + n_used // 4

    def nrm(k, shape, scale):
        return jax.random.normal(k, shape, jnp.float32) * scale

    perm = jax.random.permutation(ks[0], n_phys)
    page_table = perm[:n_used].reshape(DEC_BATCH, n_pages).astype(jnp.int32)
    lb_noise = nrm(ks[1], (DEPTH + 1, HG_QK_WIDTH), 0.3)
    hgrn_lb_logits = lb_noise.at[DEPTH].add(1.0)
    return {
        "x_prompt": nrm(ks[2], (BATCH, SEQ, D_MODEL), 1.0),
        "x_sample": nrm(ks[3], (DEC_BATCH, DEC_SEQ, D_MODEL), 1.0),
        "p_prompt": nrm(ks[4], (DEPTH, BATCH, SEQ, PLE_DIM), 1.0),
        "p_sample": nrm(ks[5], (DEPTH, DEC_BATCH, DEC_SEQ, PLE_DIM), 1.0),
        "cache_k": nrm(ks[6], (DEPTH, n_phys, PAGE_SIZE, N_HEADS, HEAD_DIM), 1.0),
        "cache_v": nrm(ks[7], (DEPTH, n_phys, PAGE_SIZE, N_HEADS, HEAD_DIM), 1.0),
        "cache_kidx": nrm(ks[8], (DEPTH, n_phys, PAGE_SIZE, IDX_DIM), 1.0),
        "state_hgrn": nrm(ks[9], (DEPTH, DEC_BATCH, HG_HEADS, HG_DK, HG_DV), 0.5),
        "page_table": page_table,
        "g_pre": 1.0 + nrm(ks[10], (DEPTH, D_MODEL), 0.05),
        "g_post": 1.0 + nrm(ks[11], (DEPTH, D_MODEL), 0.05),
        "w_in": nrm(ks[12], (DEPTH, D_MODEL, N_IN), D_MODEL ** -0.5),
        "hgrn_lb_logits": hgrn_lb_logits,
        "hgrn_norm_g": 1.0 + nrm(ks[13], (DEPTH, HG_DV), 0.05),
        "w_branch_a": nrm(ks[14], (DEPTH, ATT_WIDTH, D_MODEL), ATT_WIDTH ** -0.5),
        "w_branch_b": nrm(ks[15], (DEPTH, HG_WIDTH, D_MODEL), HG_WIDTH ** -0.5),
        "w_out": nrm(ks[16], (DEPTH, D_MODEL, D_MODEL), D_MODEL ** -0.5),
        "w_ple_gate": nrm(ks[17], (DEPTH, D_MODEL, D_MODEL), D_MODEL ** -0.5),
        "w_ple_proj": nrm(ks[18], (DEPTH, PLE_DIM, D_MODEL), PLE_DIM ** -0.5),
    }


def reference(x_prompt, x_sample, p_prompt, p_sample, cache_k, cache_v, cache_kidx, state_hgrn, page_table,
              g_pre, g_post, w_in, hgrn_lb_logits, hgrn_norm_g, w_branch_a, w_branch_b, w_out,
              w_ple_gate, w_ple_proj):
    lb_all = jnp.cumsum(jax.nn.softmax(hgrn_lb_logits.astype(jnp.float32), axis=0), axis=0)
    hp, hs = x_prompt, x_sample
    Bp = x_prompt.shape[0]
    kp_l, vp_l, kip_l, sp_l, ks_l, vs_l, kis_l, ss_l = [], [], [], [], [], [], [], []
    for i in range(DEPTH):
        lb = lb_all[i].reshape(HG_HEADS, HG_DK)
        wts = (g_post[i], hgrn_norm_g[i], w_branch_a[i], w_branch_b[i], w_out[i], w_ple_gate[i], w_ple_proj[i])
        mp = mixer_inputs(rmsnorm(hp, g_pre[i]), w_in[i], lb)
        att_p = prompt_sparse_attention(mp['q'], mp['k'], mp['v'], mp['qi'], mp['wi'], mp['ki'])
        s0 = jnp.zeros((Bp, HG_HEADS, HG_DK, HG_DV), jnp.float32)
        hgo_p, S_p = hgrn2_chunkwise(mp['hq'], mp['hk'], mp['hv'], mp['logf'], s0)
        hp = layer_output(hp, att_p, hgo_p, mp, p_prompt[i], *wts)
        ms = mixer_inputs(rmsnorm(hs, g_pre[i]), w_in[i], lb)
        att_s = sample_sparse_attention(ms['q'], ms['k'], ms['v'], ms['qi'], ms['wi'], ms['ki'],
                                        cache_k[i], cache_v[i], cache_kidx[i], page_table)
        hgo_s, S_s = hgrn2_chunkwise(ms['hq'], ms['hk'], ms['hv'], ms['logf'], state_hgrn[i])
        hs = layer_output(hs, att_s, hgo_s, ms, p_sample[i], *wts)
        kp_l.append(mp['k']); vp_l.append(mp['v']); kip_l.append(mp['ki']); sp_l.append(S_p.astype(state_hgrn.dtype))
        ks_l.append(ms['k']); vs_l.append(ms['v']); kis_l.append(ms['ki']); ss_l.append(S_s.astype(state_hgrn.dtype))
    y_prompt, y_sample = hp, hs
    k_prompt, v_prompt, kidx_prompt, hgrn_prompt = jnp.stack(kp_l), jnp.stack(vp_l), jnp.stack(kip_l), jnp.stack(sp_l)
    k_sample, v_sample, kidx_sample, hgrn_sample = jnp.stack(ks_l), jnp.stack(vs_l), jnp.stack(kis_l), jnp.stack(ss_l)
    return (y_prompt, y_sample, k_prompt, v_prompt, kidx_prompt, hgrn_prompt, k_sample, v_sample, kidx_sample, hgrn_sample)
```

```python
import functools

import jax
import jax.numpy as jnp
from jax import lax
from jax.experimental import pallas as pl
from jax.experimental.pallas import tpu as pltpu

F32 = jnp.float32
BF16 = jnp.bfloat16
I32 = jnp.int32

N_HEADS = 8
HEAD_DIM = 64
ATT_WIDTH = N_HEADS * HEAD_DIM
IDX_HEADS = 8
IDX_DIM = 64
TOPK_MAX = 256
HG_HEADS = 4
HG_DK = 128
HG_DV = 128
HG_WIDTH = HG_HEADS * HG_DV
PAGE = 128
RMS_EPS = 1e-6

LANES = 128
VMEM_LIMIT = 56 * 1024 * 1024
NEG_INF = float("-inf")
M_INIT = -1e30
INT_MAG = 0x7FFFFFFF
NPAIR = N_HEADS // 2


def _cparams(sem):
    return pltpu.CompilerParams(dimension_semantics=sem, vmem_limit_bytes=VMEM_LIMIT)


def _sigmoid(x):
    return 1.0 / (1.0 + jnp.exp(-x))


def _dot(a, b):
    return jnp.dot(a, b, preferred_element_type=F32)


def _dot_nt(a, b):
    return lax.dot_general(a, b, (((1,), (1,)), ((), ())), preferred_element_type=F32)


def _dot_tn(a, b):
    return lax.dot_general(a, b, (((0,), (0,)), ((), ())), preferred_element_type=F32)


def _inproj_kernel(x_ref, g_ref, lbl_ref, wa_ref, wb_ref, wc_ref, *refs, layer, key_major):
    x = x_ref[...]
    var = jnp.mean(x * x, axis=-1, keepdims=True)
    xn = ((x * lax.rsqrt(var + RMS_EPS)) * g_ref[...]).astype(BF16)

    def proj(w_ref, lo, hi):
        return _dot(xn, w_ref[:, lo:hi])

    w = ATT_WIDTH
    kiwi = _dot(xn, wb_ref[...])
    if key_major:
        (wt_ref, q_ref, kt_ref, ktc_ref, vt_ref, vtc_ref, ga_ref, qi_ref, kit_ref, kitc_ref, wi_ref,
         hq_ref, lf_ref, hk_ref, hv_ref, hg_ref, ma_ref, mb_ref) = refs
        kt = _dot_nt(wt_ref[0:w, :], xn)
        kt_ref[0] = kt
        ktc_ref[0, 0] = kt.astype(BF16)
        vt = _dot_nt(wt_ref[w:2 * w, :], xn)
        vt_ref[0] = vt
        vtc_ref[0, 0] = vt.astype(BF16)
        kit = _dot_nt(wt_ref[2 * w:2 * w + IDX_DIM, :], xn)
        kit_ref[0] = kit
        kitc_ref[0, 0] = kit.astype(BF16)
    else:
        (q_ref, k_ref, v_ref, ga_ref, qi_ref, ki_ref, wi_ref,
         hq_ref, lf_ref, hk_ref, hv_ref, hg_ref, ma_ref, mb_ref) = refs
        k_ref[...] = proj(wa_ref, w, 2 * w)
        v_ref[...] = proj(wa_ref, 2 * w, 3 * w)
        ki_ref[...] = kiwi[:, :IDX_DIM]
    q_ref[...] = (proj(wa_ref, 0, w) * (HEAD_DIM ** -0.5)).astype(BF16)
    ga_ref[...] = proj(wa_ref, 3 * w, 4 * w)
    qi_ref[...] = (proj(wa_ref, 4 * w, 5 * w) * (IDX_DIM ** -0.5)).astype(BF16)
    wi_ref[...] = kiwi[:, IDX_DIM:IDX_DIM + IDX_HEADS] * (IDX_HEADS ** -0.5)

    lg = lbl_ref[...]
    e = jnp.exp(lg - jnp.max(lg, axis=0, keepdims=True))
    sm = e / jnp.sum(e, axis=0, keepdims=True)
    lb = jnp.sum(sm[:layer + 1], axis=0, keepdims=True)
    one_m = 1.0 - lb

    hw = HG_WIDTH
    hq = proj(wc_ref, 0, hw)
    hq_ref[...] = hq * _sigmoid(hq)
    zf = proj(wc_ref, hw, 2 * hw)
    lf_ref[...] = jnp.log(lb + one_m * _sigmoid(zf))
    hk_ref[...] = one_m * _sigmoid(-zf)
    hv_ref[...] = proj(wc_ref, 2 * hw, 3 * hw)
    hg_ref[...] = proj(wc_ref, 3 * hw, 4 * hw)
    d = ma_ref.shape[-1]
    ma_ref[...] = proj(wc_ref, 4 * hw, 4 * hw + d)
    mb_ref[...] = proj(wc_ref, 4 * hw + d, 4 * hw + 2 * d)


def _inproj(x2, g_pre, lb_logits, wa, wb, wc, layer, tm, wt=None, batch=None):
    m, d = x2.shape
    assert m % tm == 0
    row = lambda n, dt: (pl.BlockSpec((tm, n), lambda i: (i, 0)), jax.ShapeDtypeStruct((m, n), dt))
    full = lambda a: pl.BlockSpec(a.shape, lambda i: (0,) * a.ndim)
    tail = [row(HG_WIDTH, F32)] * 5 + [row(d, F32)] * 2
    ins = [x2, g_pre, lb_logits, wa, wb, wc]
    if wt is not None:
        t = m // batch
        assert t % tm == 0
        nt = t // tm
        kmaj = lambda n: (pl.BlockSpec((1, n, tm), lambda i: (i // nt, 0, i % nt)),
                          jax.ShapeDtypeStruct((batch, n, t), F32))
        chunked = lambda n: (pl.BlockSpec((1, 1, n, tm), lambda i: (i // nt, i % nt, 0, 0)),
                             jax.ShapeDtypeStruct((batch, nt, n, tm), BF16))
        outs = [row(ATT_WIDTH, BF16), kmaj(ATT_WIDTH), chunked(ATT_WIDTH), kmaj(ATT_WIDTH), chunked(ATT_WIDTH),
                row(ATT_WIDTH, F32), row(IDX_HEADS * IDX_DIM, BF16), kmaj(IDX_DIM), chunked(IDX_DIM),
                row(IDX_HEADS, F32)] + tail
        ins.append(wt)
    else:
        outs = [row(ATT_WIDTH, BF16), row(ATT_WIDTH, F32), row(ATT_WIDTH, F32), row(ATT_WIDTH, F32),
                row(IDX_HEADS * IDX_DIM, BF16), row(IDX_DIM, F32), row(IDX_HEADS, F32)] + tail
    return pl.pallas_call(
        functools.partial(_inproj_kernel, layer=layer, key_major=wt is not None),
        grid=(m // tm,),
        in_specs=[pl.BlockSpec((tm, d), lambda i: (i, 0))] + [full(a) for a in ins[1:]],
        out_specs=[o[0] for o in outs],
        out_shape=[o[1] for o in outs],
        compiler_params=_cparams(("arbitrary",)),
        name="inproj",
    )(*ins)


def _f2key(x):
    b = lax.bitcast_convert_type(x, I32)
    return b ^ ((b >> 31) & INT_MAG)


def _key2f(k):
    return lax.bitcast_convert_type(k ^ ((k >> 31) & INT_MAG), F32)


def _select_topk_to_bias(sc_ref, nkc, rows, kc, top, n_valid, mx, mn, kpos_of):
    topf = float(top)
    nl = kc // LANES

    def count(pred):
        def body(c, acc):
            ind = jnp.where(pred(sc_ref[c], c), 1.0, 0.0)
            part = ind[:, 0:LANES]
            for j in range(1, nl):
                part = part + ind[:, j * LANES:(j + 1) * LANES]
            return acc + part
        acc = lax.fori_loop(0, nkc, body, jnp.zeros((rows, LANES), F32))
        return jnp.sum(acc, axis=-1, keepdims=True)

    small = n_valid <= top
    lo0 = _f2key(mn)
    hi0 = _f2key(mx) + 1
    conv0 = jnp.logical_and(jnp.logical_not(small), lo0 + 1 >= hi0)
    done0 = jnp.logical_or(small, conv0)
    thr0 = mn
    state0 = (lo0, hi0, jnp.zeros((rows, 1), F32), thr0, done0.astype(I32), conv0.astype(I32),
              jnp.sum(1.0 - done0.astype(F32)))

    def cond(st):
        return st[-1] > 0.0

    def body(st):
        lo, hi, ghi, thr, done_i, tie_i, _ = st
        done = done_i > 0
        mid = (lo & hi) + ((lo ^ hi) >> 1)
        tmid = _key2f(mid)
        g = count(lambda x, c: x >= tmid)
        live = jnp.logical_not(done)
        is_eq = jnp.logical_and(live, g == topf)
        up = jnp.logical_and(live, g > topf)
        down = jnp.logical_and(live, g < topf)
        lo = jnp.where(up, mid, lo)
        hi = jnp.where(down, mid, hi)
        ghi = jnp.where(down, g, ghi)
        thr = jnp.where(is_eq, tmid, thr)
        done = jnp.logical_or(done, is_eq)
        conv = jnp.logical_and(jnp.logical_not(done), lo + 1 >= hi)
        thr = jnp.where(conv, _key2f(lo), thr)
        done = jnp.logical_or(done, conv)
        tie_i = jnp.where(conv, 1, tie_i)
        return (lo, hi, ghi, thr, done.astype(I32), tie_i, jnp.sum(1.0 - done.astype(F32)))

    _, _, ghi, thr, _, tie_i, _ = lax.while_loop(cond, body, state0)

    tie = tie_i > 0
    n_tie = jnp.sum(tie_i.astype(F32))

    @pl.when(n_tie > 0.0)
    def _():
        need = topf - ghi
        jlo0 = jnp.full((rows, 1), -1, I32)
        jhi0 = jnp.full((rows, 1), 0, I32) + (nkc * kc - 1)
        steps = max(1, (sc_ref.shape[0] * kc - 1).bit_length()) + 1

        def jbody(_, carry):
            jlo, jhi = carry
            jm = (jlo + jhi) >> 1
            cnt = count(lambda x, c: jnp.logical_and(x == thr, kpos_of(c) <= jm))
            ok = cnt >= need
            return jnp.where(ok, jlo, jm), jnp.where(ok, jm, jhi)

        _, jcut = lax.fori_loop(0, steps, jbody, (jlo0, jhi0))

        def kill(c, _):
            x = sc_ref[c]
            drop = jnp.logical_and(jnp.logical_and(tie, x == thr), kpos_of(c) > jcut)
            sc_ref[c] = jnp.where(drop, NEG_INF, x)
            return 0
        lax.fori_loop(0, nkc, kill, 0)

    def to_bias(c, _):
        sc_ref[c] = jnp.where(sc_ref[c] >= thr, 0.0, NEG_INF)
        return 0
    lax.fori_loop(0, nkc, to_bias, 0)


def _indexer_scores(d, wcols, rows):
    acc = wcols[0] * jnp.maximum(d[0:rows], 0.0)
    for h in range(1, IDX_HEADS):
        acc = acc + wcols[h] * jnp.maximum(d[h * rows:(h + 1) * rows], 0.0)
    return acc


def _masked_scores(acc, valid, mx, mn):
    mx = jnp.maximum(mx, jnp.max(jnp.where(valid, acc, NEG_INF), axis=-1, keepdims=True))
    mn = jnp.minimum(mn, jnp.min(jnp.where(valid, acc, -NEG_INF), axis=-1, keepdims=True))
    return jnp.where(valid, acc, NEG_INF), mx, mn


def _split_pair(qp, lane):
    return jnp.concatenate([jnp.where(lane < HEAD_DIM, qp, 0.0), jnp.where(lane >= HEAD_DIM, qp, 0.0)], axis=0)


def _merge_pair(x, lane, rows):
    return jnp.where(lane < HEAD_DIM, x[0:rows], x[rows:2 * rows])


def _attn_prompt_kernel(qi_ref, wi_ref, kit_ref, q_ref, kt_ref, vt_ref, o_ref,
                        sc_ref, qis_ref, qs_ref, *, qb, kc, top):
    i = pl.program_id(1)
    nkc = ((i + 1) * qb + kc - 1) // kc

    for h in range(IDX_HEADS):
        qis_ref[h * qb:(h + 1) * qb, :] = qi_ref[0, :, h * IDX_DIM:(h + 1) * IDX_DIM]
    lane = lax.broadcasted_iota(I32, (qb, LANES), 1)
    for hp in range(NPAIR):
        qp = q_ref[0, :, hp * LANES:(hp + 1) * LANES].astype(F32)
        qs_ref[hp] = _split_pair(qp, lane).astype(BF16)

    wi = wi_ref[0]
    wcols = [wi[:, h:h + 1] for h in range(IDX_HEADS)]
    qpos = i * qb + lax.broadcasted_iota(I32, (qb, 1), 0)
    col = lax.broadcasted_iota(I32, (1, kc), 1)

    def kpos_of(c):
        return c * kc + col

    def score(c, carry):
        acc = _indexer_scores(_dot(qis_ref[...], kit_ref[0, c]), wcols, qb)
        sc_ref[c], mx, mn = _masked_scores(acc, kpos_of(c) <= qpos, *carry)
        return mx, mn

    mx, mn = lax.fori_loop(0, nkc, score,
                           (jnp.full((qb, 1), NEG_INF, F32), jnp.full((qb, 1), -NEG_INF, F32)))
    _select_topk_to_bias(sc_ref, nkc, qb, kc, top, qpos + 1, mx, mn, kpos_of)

    for hp in range(NPAIR):
        qs = qs_ref[hp]
        pair = slice(hp * LANES, (hp + 1) * LANES)

        def att(c, carry, qs=qs, pair=pair):
            m, l, acc = carry
            bias = sc_ref[c]
            s = _dot(qs, kt_ref[0, c, pair, :]) + jnp.concatenate([bias, bias], axis=0)
            m_new = jnp.maximum(m, jnp.max(s, axis=-1, keepdims=True))
            alpha = jnp.exp(m - m_new)
            p = jnp.exp(s - m_new)
            l = alpha * l + jnp.sum(p, axis=-1, keepdims=True)
            acc = alpha * acc + _dot_nt(p.astype(BF16), vt_ref[0, c, pair, :])
            return m_new, l, acc

        m0 = jnp.full((2 * qb, 1), M_INIT, F32)
        l0 = jnp.zeros((2 * qb, 1), F32)
        a0 = jnp.zeros((2 * qb, LANES), F32)
        _, l, acc = lax.fori_loop(0, nkc, att, (m0, l0, a0))
        o_ref[0, :, pair] = _merge_pair(acc / l, lane, qb)


def _attn_prompt(qi, wi, kitc, q, ktc, vtc, qb):
    b, t, _ = q.shape
    nc, kc = ktc.shape[1], ktc.shape[3]
    top = min(TOPK_MAX, t // 4)
    assert t % qb == 0 and nc * kc == t and kc % qb == 0 and kc % LANES == 0
    blk = lambda n: pl.BlockSpec((1, qb, n), lambda bi, i: (bi, i, 0))
    whole = lambda n: pl.BlockSpec((1, nc, n, kc), lambda bi, i: (bi, 0, 0, 0))
    return pl.pallas_call(
        functools.partial(_attn_prompt_kernel, qb=qb, kc=kc, top=top),
        grid=(b, t // qb),
        in_specs=[blk(IDX_HEADS * IDX_DIM), blk(IDX_HEADS), whole(IDX_DIM),
                  blk(ATT_WIDTH), whole(ATT_WIDTH), whole(ATT_WIDTH)],
        out_specs=blk(ATT_WIDTH),
        out_shape=jax.ShapeDtypeStruct((b, t, ATT_WIDTH), F32),
        scratch_shapes=[pltpu.VMEM((nc, qb, kc), F32),
                        pltpu.VMEM((IDX_HEADS * qb, IDX_DIM), BF16),
                        pltpu.VMEM((NPAIR, 2 * qb, LANES), BF16)],
        compiler_params=_cparams(("arbitrary", "arbitrary")),
        name="attn_prompt",
    )(qi, wi, kitc, q, ktc, vtc)


def _index_sample_kernel(pt_ref, qi_ref, wi_ref, kin_ref, cki_hbm, bias_ref,
                         kibuf_ref, qis_ref, knew_ref, sem, *, tq, npages, top):
    b = pl.program_id(0)
    past = npages * PAGE

    def page_copy(p):
        return pltpu.make_async_copy(cki_hbm.at[pt_ref[b, p]], kibuf_ref.at[p], sem)

    def start(p, _):
        page_copy(p).start()
        return 0
    lax.fori_loop(0, npages, start, 0)

    knew_ref[...] = jnp.zeros(knew_ref.shape, F32)
    knew_ref[0:tq, :] = kin_ref[0]
    qi = qi_ref[0].astype(F32)
    for h in range(IDX_HEADS):
        qis_ref[h * tq:(h + 1) * tq, :] = qi[:, h * IDX_DIM:(h + 1) * IDX_DIM]
    wi = wi_ref[0]
    wcols = [wi[:, h:h + 1] for h in range(IDX_HEADS)]
    qpos = past + lax.broadcasted_iota(I32, (tq, 1), 0)
    col = lax.broadcasted_iota(I32, (1, PAGE), 1)

    def kpos_of(c):
        return c * PAGE + col

    def wait(p, _):
        page_copy(p).wait()
        return 0
    lax.fori_loop(0, npages, wait, 0)

    qis = qis_ref[...].astype(BF16)
    always = col >= 0

    def score(c, carry):
        acc = _indexer_scores(_dot(qis, kibuf_ref[c].astype(BF16)), wcols, tq)
        bias_ref[0, c], mx, mn = _masked_scores(acc, always, *carry)
        return mx, mn

    carry = lax.fori_loop(0, npages, score,
                          (jnp.full((tq, 1), NEG_INF, F32), jnp.full((tq, 1), -NEG_INF, F32)))
    acc = _indexer_scores(_dot_nt(qis, knew_ref[...].astype(BF16)), wcols, tq)
    bias_ref[0, npages], mx, mn = _masked_scores(acc, kpos_of(npages) <= qpos, *carry)
    _select_topk_to_bias(bias_ref.at[0], npages + 1, tq, PAGE, top, qpos + 1, mx, mn, kpos_of)


def _index_sample(page_table, qi, wi, ki_new, cki_t):
    b, tq, _ = qi.shape
    npages = page_table.shape[1]
    top = min(TOPK_MAX, (npages * PAGE + tq) // 4)
    nkc = npages + 1
    assert tq <= PAGE
    blk = lambda n: pl.BlockSpec((1, tq, n), lambda bi, pt: (bi, 0, 0))
    grid_spec = pltpu.PrefetchScalarGridSpec(
        num_scalar_prefetch=1,
        grid=(b,),
        in_specs=[blk(IDX_HEADS * IDX_DIM), blk(IDX_HEADS), blk(IDX_DIM),
                  pl.BlockSpec(memory_space=pl.ANY)],
        out_specs=pl.BlockSpec((1, nkc, tq, PAGE), lambda bi, pt: (bi, 0, 0, 0)),
        scratch_shapes=[pltpu.VMEM((npages, IDX_DIM, PAGE), F32),
                        pltpu.VMEM((IDX_HEADS * tq, IDX_DIM), F32),
                        pltpu.VMEM((PAGE, IDX_DIM), F32),
                        pltpu.SemaphoreType.DMA(())],
    )
    return pl.pallas_call(
        functools.partial(_index_sample_kernel, tq=tq, npages=npages, top=top),
        grid_spec=grid_spec,
        out_shape=jax.ShapeDtypeStruct((b, nkc, tq, PAGE), F32),
        compiler_params=_cparams(("arbitrary",)),
        name="index_sample",
    )(page_table, qi, wi, ki_new, cki_t)


def _attn_sample_kernel(pt_ref, q_ref, kn_ref, vn_ref, bias_ref, ck_hbm, cv_hbm, o_ref,
                        kbuf_ref, vbuf_ref, qs_ref, knew_ref, vnew_ref, m_ref, l_ref, acc_ref, sems,
                        *, tq, gp, npages, nb):
    b = pl.program_id(0)
    g = pl.program_id(1)
    ng = npages // gp
    step = b * ng + g
    slot = step % 2

    def group_copies(bb, gg, sl):
        out = []
        for j in range(gp):
            page = pt_ref[bb, gg * gp + j]
            out.append(pltpu.make_async_copy(ck_hbm.at[page], kbuf_ref.at[sl, j], sems.at[0, sl]))
            out.append(pltpu.make_async_copy(cv_hbm.at[page], vbuf_ref.at[sl, j], sems.at[1, sl]))
        return out

    @pl.when(step == 0)
    def _():
        for cp in group_copies(b, g, slot):
            cp.start()

    @pl.when(step + 1 < nb * ng)
    def _():
        nxt = step + 1
        for cp in group_copies(nxt // ng, nxt % ng, 1 - slot):
            cp.start()

    lane = lax.broadcasted_iota(I32, (tq, LANES), 1)

    @pl.when(g == 0)
    def _():
        q = q_ref[0].astype(F32)
        for hp in range(NPAIR):
            qs_ref[hp] = _split_pair(q[:, hp * LANES:(hp + 1) * LANES], lane)
        m_ref[...] = jnp.full(m_ref.shape, M_INIT, F32)
        l_ref[...] = jnp.zeros(l_ref.shape, F32)
        acc_ref[...] = jnp.zeros(acc_ref.shape, F32)

    for cp in group_copies(b, g, slot):
        cp.wait()

    def bias2(c):
        bias = bias_ref[0, c]
        return jnp.concatenate([bias, bias], axis=0)

    def update(hp, s_list, pv_of):
        m = m_ref[hp]
        m_new = m
        for s in s_list:
            m_new = jnp.maximum(m_new, jnp.max(s, axis=-1, keepdims=True))
        alpha = jnp.exp(m - m_new)
        l = alpha * l_ref[hp]
        acc = alpha * acc_ref[hp]
        for j, s in enumerate(s_list):
            p = jnp.exp(s - m_new)
            l = l + jnp.sum(p, axis=-1, keepdims=True)
            acc = acc + pv_of(j, p.astype(BF16))
        m_ref[hp] = m_new
        l_ref[hp] = l
        acc_ref[hp] = acc

    for hp in range(NPAIR):
        qs = qs_ref[hp].astype(BF16)
        pair = slice(hp * LANES, (hp + 1) * LANES)
        s_list = [_dot(qs, kbuf_ref[slot, j, pair, :].astype(BF16)) + bias2(g * gp + j) for j in range(gp)]
        update(hp, s_list, lambda j, p, pair=pair: _dot_nt(p, vbuf_ref[slot, j, pair, :].astype(BF16)))

    @pl.when(g == ng - 1)
    def _():
        knew_ref[...] = jnp.zeros(knew_ref.shape, F32)
        vnew_ref[...] = jnp.zeros(vnew_ref.shape, F32)
        knew_ref[0:tq, :] = kn_ref[0]
        vnew_ref[0:tq, :] = vn_ref[0]
        for hp in range(NPAIR):
            qs = qs_ref[hp].astype(BF16)
            pair = slice(hp * LANES, (hp + 1) * LANES)
            s = _dot_nt(qs, knew_ref[:, pair].astype(BF16)) + bias2(npages)
            update(hp, [s], lambda j, p, pair=pair: _dot(p, vnew_ref[:, pair].astype(BF16)))
            o_ref[0, :, pair] = _merge_pair(acc_ref[hp] / l_ref[hp], lane, tq)


def _attn_sample(page_table, q, k_new, v_new, bias, ck_t, cv_t, gp):
    b, tq, _ = q.shape
    npages = page_table.shape[1]
    assert npages % gp == 0 and bias.shape[1] == npages + 1 and tq <= PAGE
    blk = lambda n: pl.BlockSpec((1, tq, n), lambda bi, gi, pt: (bi, 0, 0))
    grid_spec = pltpu.PrefetchScalarGridSpec(
        num_scalar_prefetch=1,
        grid=(b, npages // gp),
        in_specs=[blk(ATT_WIDTH), blk(ATT_WIDTH), blk(ATT_WIDTH),
                  pl.BlockSpec((1, npages + 1, tq, PAGE), lambda bi, gi, pt: (bi, 0, 0, 0)),
                  pl.BlockSpec(memory_space=pl.ANY), pl.BlockSpec(memory_space=pl.ANY)],
        out_specs=blk(ATT_WIDTH),
        scratch_shapes=[pltpu.VMEM((2, gp, ATT_WIDTH, PAGE), F32),
                        pltpu.VMEM((2, gp, ATT_WIDTH, PAGE), F32),
                        pltpu.VMEM((NPAIR, 2 * tq, LANES), F32),
                        pltpu.VMEM((PAGE, ATT_WIDTH), F32),
                        pltpu.VMEM((PAGE, ATT_WIDTH), F32),
                        pltpu.VMEM((NPAIR, 2 * tq, 1), F32),
                        pltpu.VMEM((NPAIR, 2 * tq, 1), F32),
                        pltpu.VMEM((NPAIR, 2 * tq, LANES), F32),
                        pltpu.SemaphoreType.DMA((2, 2))],
    )
    return pl.pallas_call(
        functools.partial(_attn_sample_kernel, tq=tq, gp=gp, npages=npages, nb=b),
        grid_spec=grid_spec,
        out_shape=jax.ShapeDtypeStruct((b, tq, ATT_WIDTH), F32),
        compiler_params=_cparams(("arbitrary", "arbitrary")),
        name="attn_sample",
    )(page_table, q, k_new, v_new, bias, ck_t, cv_t)


def _cumsum_rows(x):
    n = x.shape[0]
    row = lax.broadcasted_iota(I32, x.shape, 0)
    sh = 1
    while sh < n:
        x = x + jnp.where(row >= sh, pltpu.roll(x, sh, 0), 0.0)
        sh *= 2
    return x


def _hgrn_kernel(hq_ref, lf_ref, hk_ref, hv_ref, hg_ref, s0_ref, ng_ref, o_ref, sout_ref,
                 st_ref, *, tb, c, sb):
    t = pl.program_id(2)
    nt = pl.num_programs(2)

    @pl.when(t == 0)
    def _():
        st_ref[...] = s0_ref[0, 0].T

    row_sb = lax.broadcasted_iota(I32, (sb, 1), 0)
    row_c = lax.broadcasted_iota(I32, (c, 1), 0)
    nsb = c // sb

    for j in range(tb // c):
        rows = slice(j * c, (j + 1) * c)
        lf = lf_ref[0, rows, :]
        q = hq_ref[0, rows, :]
        k = hk_ref[0, rows, :]
        v = hv_ref[0, rows, :]
        cum = _cumsum_rows(lf)
        last = cum[c - 1:c, :]
        st = st_ref[...]
        o = _dot_nt((q * jnp.exp(cum)).astype(BF16), st.astype(BF16))

        vb = v.astype(BF16)
        o_parts = []
        a_rows = []
        for i in range(nsb):
            r = slice(i * sb, (i + 1) * sb)
            cum_i, q_i = cum[r], q[r]
            od = jnp.zeros((sb, HG_DV), F32)
            for s in range(sb):
                sr = i * sb + s
                e = jnp.exp(jnp.where(row_sb >= s, cum_i - cum[sr:sr + 1, :], NEG_INF))
                a_col = jnp.sum(e * (q_i * k[sr:sr + 1, :]), axis=-1, keepdims=True)
                od = od + a_col * v[sr:sr + 1, :]
            o_parts.append(od)
            if i > 0:
                c0 = cum[i * sb:i * sb + 1, :] - lf[i * sb:i * sb + 1, :]
                qs = (q_i * jnp.exp(cum_i - c0)).astype(BF16)
                ks = (k * jnp.exp(jnp.where(row_c < i * sb, c0 - cum, NEG_INF))).astype(BF16)
                a_rows.append(_dot_nt(qs, ks))
            elif nsb > 1:
                a_rows.append(jnp.zeros((sb, c), F32))
        o = o + jnp.concatenate(o_parts, axis=0)
        if nsb > 1:
            o = o + _dot(jnp.concatenate(a_rows, axis=0).astype(BF16), vb)

        kd = (k * jnp.exp(last - cum)).astype(BF16)
        st_ref[...] = st * jnp.exp(last) + _dot_tn(vb, kd)

        ms = jnp.mean(o * o, axis=-1, keepdims=True)
        hg = hg_ref[0, rows, :]
        o_ref[0, rows, :] = (((o * lax.rsqrt(ms + RMS_EPS)) * ng_ref[...]) * (hg * _sigmoid(hg))).astype(BF16)

    @pl.when(t == nt - 1)
    def _():
        sout_ref[0, 0] = st_ref[...].T


def _hgrn(hq, lf, hk, hv, hg, s0, norm_g, tb, c):
    b, t, _ = hq.shape
    sb = 16
    assert t % tb == 0 and tb % c == 0 and c % sb == 0
    blk = pl.BlockSpec((1, tb, HG_DK), lambda bi, h, ti: (bi, ti, h))
    sblk = pl.BlockSpec((1, 1, HG_DK, HG_DV), lambda bi, h, ti: (bi, h, 0, 0))
    return pl.pallas_call(
        functools.partial(_hgrn_kernel, tb=tb, c=c, sb=sb),
        grid=(b, HG_HEADS, t // tb),
        in_specs=[blk, blk, blk, blk, blk, sblk, pl.BlockSpec((1, HG_DV), lambda bi, h, ti: (0, 0))],
        out_specs=[blk, sblk],
        out_shape=[jax.ShapeDtypeStruct((b, t, HG_WIDTH), BF16),
                   jax.ShapeDtypeStruct((b, HG_HEADS, HG_DK, HG_DV), F32)],
        scratch_shapes=[pltpu.VMEM((HG_DV, HG_DK), F32)],
        compiler_params=_cparams(("arbitrary", "arbitrary", "arbitrary")),
        name="hgrn",
    )(hq, lf, hk, hv, hg, s0, norm_g)


def _merge_kernel(x_ref, att_ref, ga_ref, hb_ref, ma_ref, mb_ref, p_ref,
                  wa_ref, wb_ref, wo_ref, wpg_ref, wpp_ref, gp_ref, o_ref):
    ga = ga_ref[...]
    ya = _dot((att_ref[...] * (ga * _sigmoid(ga))).astype(BF16), wa_ref[...])
    yb = _dot(hb_ref[...], wb_ref[...])
    mixed = _sigmoid(ma_ref[...]) * ya + _sigmoid(mb_ref[...]) * yb
    u = _dot(mixed.astype(BF16), wo_ref[...])
    ms = jnp.mean(u * u, axis=-1, keepdims=True)
    h1 = x_ref[...] + (u * lax.rsqrt(ms + RMS_EPS)) * gp_ref[...]
    gate = _sigmoid(_dot(h1.astype(BF16), wpg_ref[...]))
    o_ref[...] = h1 + gate * _dot(p_ref[...].astype(BF16), wpp_ref[...])


def _merge(x2, att, ga, hb, ma, mb, p2, w_a, w_b, w_o, w_pg, w_pp, g_post, tm):
    m, d = x2.shape
    assert m % tm == 0
    row = lambda a: pl.BlockSpec((tm, a.shape[1]), lambda i: (i, 0))
    full = lambda a: pl.BlockSpec(a.shape, lambda i: (0, 0))
    acts = (x2, att, ga, hb, ma, mb, p2)
    wts = (w_a, w_b, w_o, w_pg, w_pp, g_post)
    return pl.pallas_call(
        _merge_kernel,
        grid=(m // tm,),
        in_specs=[row(a) for a in acts] + [full(w) for w in wts],
        out_specs=pl.BlockSpec((tm, d), lambda i: (i, 0)),
        out_shape=jax.ShapeDtypeStruct((m, d), F32),
        compiler_params=_cparams(("arbitrary",)),
        name="merge",
    )(*acts, *wts)


def _pick_rows(m, pref):
    return pref if m % pref == 0 else m


def kernel(x_prompt, x_sample, p_prompt, p_sample, cache_k, cache_v, cache_kidx, state_hgrn, page_table,
           g_pre, g_post, w_in, hgrn_lb_logits, hgrn_norm_g, w_branch_a, w_branch_b, w_out,
           w_ple_gate, w_ple_proj):
    depth = w_in.shape[0]
    bp, tp, d = x_prompt.shape
    bs, ts, _ = x_sample.shape
    nphys = cache_k.shape[1]
    hp, hs = x_prompt, x_sample
    outs = [[] for _ in range(8)]
    a_end = 5 * ATT_WIDTH
    b_end = a_end + IDX_DIM + IDX_HEADS

    for i in range(depth):
        wa = w_in[i][:, :a_end].astype(BF16)
        wb = w_in[i][:, a_end:b_end].astype(BF16)
        wc = w_in[i][:, b_end:].astype(BF16)
        wt = jnp.concatenate([w_in[i][:, ATT_WIDTH:3 * ATT_WIDTH], w_in[i][:, a_end:a_end + IDX_DIM]],
                             axis=1).T.astype(BF16)
        g_pre_i = g_pre[i].reshape(1, d)
        wts = (w_branch_a[i].astype(BF16), w_branch_b[i].astype(BF16), w_out[i].astype(BF16),
               w_ple_gate[i].astype(BF16), w_ple_proj[i].astype(BF16), g_post[i].reshape(1, d))
        norm_g = hgrn_norm_g[i].reshape(1, HG_DV)

        mp = bp * tp
        kc = _pick_rows(tp, 256)
        (q, kt, ktc, vt, vtc, ga, qi, kit, kitc, wi, hq, lf, hk, hv, hg, ma, mb) = _inproj(
            hp.reshape(mp, d), g_pre_i, hgrn_lb_logits, wa, wb, wc, i, kc, wt=wt, batch=bp)
        r3 = lambda a, b_=bp, t_=tp: a.reshape(b_, t_, a.shape[-1])
        att = _attn_prompt(r3(qi), r3(wi), kitc, r3(q), ktc, vtc, qb=128)
        s0 = jnp.zeros((bp, HG_HEADS, HG_DK, HG_DV), F32)
        hb, s_p = _hgrn(r3(hq), r3(lf), r3(hk), r3(hv), r3(hg), s0, norm_g, tb=_pick_rows(tp, 256), c=64)
        y = _merge(hp.reshape(mp, d), att.reshape(mp, ATT_WIDTH), ga, hb.reshape(mp, HG_WIDTH), ma, mb,
                   p_prompt[i].reshape(mp, -1), *wts, _pick_rows(mp, 256))
        hp = y.reshape(bp, tp, d)
        outs[0].append(kt.reshape(bp, N_HEADS, HEAD_DIM, tp).transpose(0, 3, 1, 2))
        outs[1].append(vt.reshape(bp, N_HEADS, HEAD_DIM, tp).transpose(0, 3, 1, 2))
        outs[2].append(kit.transpose(0, 2, 1))
        outs[3].append(s_p.astype(state_hgrn.dtype))

        ms = bs * ts
        (q, k, v, ga, qi, ki, wi, hq, lf, hk, hv, hg, ma, mb) = _inproj(
            hs.reshape(ms, d), g_pre_i, hgrn_lb_logits, wa, wb, wc, i, _pick_rows(ms, 256))
        r3 = lambda a, b_=bs, t_=ts: a.reshape(b_, t_, a.shape[-1])
        cki_t = cache_kidx[i].transpose(0, 2, 1)
        ck_t = cache_k[i].transpose(0, 2, 3, 1).reshape(nphys, ATT_WIDTH, PAGE)
        cv_t = cache_v[i].transpose(0, 2, 3, 1).reshape(nphys, ATT_WIDTH, PAGE)
        bias = _index_sample(page_table, r3(qi), r3(wi), r3(ki), cki_t)
        att = _attn_sample(page_table, r3(q), r3(k), r3(v), bias, ck_t, cv_t, gp=8)
        tpad = -ts % 16
        pad = lambda a: jnp.pad(r3(a), ((0, 0), (0, tpad), (0, 0)))
        hb, s_s = _hgrn(pad(hq), pad(lf), pad(hk), pad(hv), pad(hg), state_hgrn[i], norm_g,
                        tb=ts + tpad, c=ts + tpad)
        y = _merge(hs.reshape(ms, d), att.reshape(ms, ATT_WIDTH), ga, hb[:, :ts].reshape(ms, HG_WIDTH), ma, mb,
                   p_sample[i].reshape(ms, -1), *wts, _pick_rows(ms, 256))
        hs = y.reshape(bs, ts, d)
        outs[4].append(k.reshape(bs, ts, N_HEADS, HEAD_DIM))
        outs[5].append(v.reshape(bs, ts, N_HEADS, HEAD_DIM))
        outs[6].append(ki.reshape(bs, ts, IDX_DIM))
        outs[7].append(s_s.astype(state_hgrn.dtype))

    st = [jnp.stack(o) for o in outs]
    return (hp, hs, st[0], st[1], st[2], st[3], st[4], st[5], st[6], st[7])
```

```python
import functools

import jax
import jax.numpy as jnp
from jax import lax
from jax.experimental import pallas as pl
from jax.experimental.pallas import tpu as pltpu

F32 = jnp.float32
BF16 = jnp.bfloat16
I32 = jnp.int32

N_HEADS = 8
HEAD_DIM = 64
ATT_WIDTH = N_HEADS * HEAD_DIM
IDX_HEADS = 8
IDX_DIM = 64
IDX_WIDTH = IDX_HEADS * IDX_DIM
TOPK_MAX = 256
HG_HEADS = 4
HG_DK = 128
HG_DV = 128
HG_WIDTH = HG_HEADS * HG_DV
PAGE = 128
RMS_EPS = 1e-6

LANES = 128
SUBLANES = 8
VMEM_LIMIT = 56 * 1024 * 1024
NEG_INF = float("-inf")
M_INIT = -1e30
INT_MAG = 0x7FFFFFFF
NPAIR = N_HEADS // 2
LOG2E = 1.4426950408889634
FLOAT_BISECT_STEPS = 12


def _cparams(sem):
    return pltpu.CompilerParams(dimension_semantics=sem, vmem_limit_bytes=VMEM_LIMIT)


def _sigmoid(x):
    return 1.0 / (1.0 + jnp.exp(-x))


def _dot(a, b):
    return jnp.dot(a, b, preferred_element_type=F32)


def _dot_nt(a, b):
    return lax.dot_general(a, b, (((1,), (1,)), ((), ())), preferred_element_type=F32)


def _dot_tn(a, b):
    return lax.dot_general(a, b, (((0,), (0,)), ((), ())), preferred_element_type=F32)


def _tree(op, xs):
    xs = list(xs)
    while len(xs) > 1:
        xs = [op(xs[i], xs[i + 1]) if i + 1 < len(xs) else xs[i] for i in range(0, len(xs), 2)]
    return xs[0]


def _fold_rows(x, op):
    return _tree(op, [x[r:r + SUBLANES] for r in range(0, x.shape[0], SUBLANES)])


def _normed(x_ref, g_ref):
    x = x_ref[...]
    var = jnp.mean(x * x, axis=-1, keepdims=True)
    return ((x * lax.rsqrt(var + RMS_EPS)) * g_ref[...]).astype(BF16)


def _hgrn_and_merge_gates(xn, w_ref, off, lbl_ref, layer, hq_ref, lf_ref, hk_ref, hv_ref, hg_ref, ma_ref, mb_ref):
    def proj(lo, hi):
        return _dot(xn, w_ref[:, off + lo:off + hi])

    lg = lbl_ref[...]
    e = jnp.exp(lg - jnp.max(lg, axis=0, keepdims=True))
    sm = e / jnp.sum(e, axis=0, keepdims=True)
    lb = jnp.sum(sm[:layer + 1], axis=0, keepdims=True)
    one_m = 1.0 - lb

    hw = HG_WIDTH
    hq = proj(0, hw)
    hq_ref[...] = hq * _sigmoid(hq)
    zf = proj(hw, 2 * hw)
    lf_ref[...] = jnp.log(lb + one_m * _sigmoid(zf))
    hk_ref[...] = one_m * _sigmoid(-zf)
    hv_ref[...] = proj(2 * hw, 3 * hw)
    hg_ref[...] = proj(3 * hw, 4 * hw)
    d = ma_ref.shape[-1]
    ma_ref[...] = proj(4 * hw, 4 * hw + d)
    mb_ref[...] = proj(4 * hw + d, 4 * hw + 2 * d)


def _inproj_sample_kernel(x_ref, g_ref, lbl_ref, wa_ref, wb_ref, wc_ref,
                          q_ref, k_ref, v_ref, ga_ref, qi_ref, ki_ref, wi_ref, *tail, layer):
    xn = _normed(x_ref, g_ref)
    w = ATT_WIDTH
    q_ref[...] = (_dot(xn, wa_ref[:, 0:w]) * (HEAD_DIM ** -0.5)).astype(BF16)
    k_ref[...] = _dot(xn, wa_ref[:, w:2 * w])
    v_ref[...] = _dot(xn, wa_ref[:, 2 * w:3 * w])
    ga_ref[...] = _dot(xn, wa_ref[:, 3 * w:4 * w])
    qi_ref[...] = (_dot(xn, wa_ref[:, 4 * w:4 * w + IDX_WIDTH]) * (IDX_DIM ** -0.5)).astype(BF16)
    kiwi = _dot(xn, wb_ref[...])
    ki_ref[...] = kiwi[:, :IDX_DIM]
    wi_ref[...] = kiwi[:, IDX_DIM:IDX_DIM + IDX_HEADS] * (IDX_HEADS ** -0.5)
    _hgrn_and_merge_gates(xn, wc_ref, 0, lbl_ref, layer, *tail)


def _inproj_prompt_kernel(x_ref, g_ref, lbl_ref, wt_ref, wtok_ref, wb_ref,
                          qt_ref, kt_ref, ktok_ref, vt_ref, vtc_ref, ga_ref, qit_ref, kit_ref, kitok_ref, wit_ref,
                          *tail, layer):
    xn = _normed(x_ref, g_ref)
    w = ATT_WIDTH

    def tproj(lo, hi):
        return _dot_nt(wt_ref[lo:hi, :], xn)

    qt_ref[0] = (tproj(0, w) * (HEAD_DIM ** -0.5 * LOG2E)).astype(BF16)
    kt_ref[0] = tproj(w, 2 * w)
    vt = tproj(2 * w, 3 * w)
    vt_ref[0] = vt
    vtc_ref[0, 0] = vt.astype(BF16)
    o = 3 * w
    qit_ref[0] = (tproj(o, o + IDX_WIDTH) * (IDX_DIM ** -0.5)).astype(BF16)
    o += IDX_WIDTH
    kit_ref[0] = tproj(o, o + IDX_DIM)
    wit_ref[0] = tproj(o + IDX_DIM, o + IDX_DIM + IDX_HEADS) * (IDX_HEADS ** -0.5)

    ktok_ref[0, 0] = _dot(xn, wtok_ref[:, 0:w]).astype(BF16)
    kitok_ref[0, 0] = _dot(xn, wb_ref[:, 0:IDX_DIM]).astype(BF16)
    ga_ref[...] = _dot(xn, wtok_ref[:, w:2 * w])
    _hgrn_and_merge_gates(xn, wtok_ref, 2 * w, lbl_ref, layer, *tail)


def _inproj_sample(x2, g_pre, lb_logits, wa, wb, wc, layer, tm):
    m, d = x2.shape
    assert m % tm == 0
    row = lambda n, dt: (pl.BlockSpec((tm, n), lambda i: (i, 0)), jax.ShapeDtypeStruct((m, n), dt))
    full = lambda a: pl.BlockSpec(a.shape, lambda i: (0,) * a.ndim)
    outs = [row(ATT_WIDTH, BF16), row(ATT_WIDTH, F32), row(ATT_WIDTH, F32), row(ATT_WIDTH, F32),
            row(IDX_WIDTH, BF16), row(IDX_DIM, F32), row(IDX_HEADS, F32)] + [row(HG_WIDTH, F32)] * 5 + [row(d, F32)] * 2
    ins = [x2, g_pre, lb_logits, wa, wb, wc]
    return pl.pallas_call(
        functools.partial(_inproj_sample_kernel, layer=layer),
        grid=(m // tm,),
        in_specs=[pl.BlockSpec((tm, d), lambda i: (i, 0))] + [full(a) for a in ins[1:]],
        out_specs=[o[0] for o in outs],
        out_shape=[o[1] for o in outs],
        compiler_params=_cparams(("arbitrary",)),
        name="inproj_sample",
    )(*ins)


def _inproj_prompt(x2, g_pre, lb_logits, wt, wtok, wb, layer, tm, batch):
    m, d = x2.shape
    t = m // batch
    assert m % tm == 0 and t % tm == 0
    nt = t // tm
    row = lambda n, dt: (pl.BlockSpec((tm, n), lambda i: (i, 0)), jax.ShapeDtypeStruct((m, n), dt))
    full = lambda a: pl.BlockSpec(a.shape, lambda i: (0,) * a.ndim)
    fmaj = lambda n, dt: (pl.BlockSpec((1, n, tm), lambda i: (i // nt, 0, i % nt)),
                          jax.ShapeDtypeStruct((batch, n, t), dt))
    fchunk = lambda n: (pl.BlockSpec((1, 1, n, tm), lambda i: (i // nt, i % nt, 0, 0)),
                        jax.ShapeDtypeStruct((batch, nt, n, tm), BF16))
    tchunk = lambda n: (pl.BlockSpec((1, 1, tm, n), lambda i: (i // nt, i % nt, 0, 0)),
                        jax.ShapeDtypeStruct((batch, nt, tm, n), BF16))
    outs = [fmaj(ATT_WIDTH, BF16), fmaj(ATT_WIDTH, F32), tchunk(ATT_WIDTH), fmaj(ATT_WIDTH, F32), fchunk(ATT_WIDTH),
            row(ATT_WIDTH, F32), fmaj(IDX_WIDTH, BF16), fmaj(IDX_DIM, F32), tchunk(IDX_DIM), fmaj(IDX_HEADS, F32)]
    outs += [row(HG_WIDTH, F32)] * 5 + [row(d, F32)] * 2
    ins = [x2, g_pre, lb_logits, wt, wtok, wb]
    return pl.pallas_call(
        functools.partial(_inproj_prompt_kernel, layer=layer),
        grid=(m // tm,),
        in_specs=[pl.BlockSpec((tm, d), lambda i: (i, 0))] + [full(a) for a in ins[1:]],
        out_specs=[o[0] for o in outs],
        out_shape=[o[1] for o in outs],
        compiler_params=_cparams(("arbitrary",)),
        name="inproj_prompt",
    )(*ins)


def _f2key(x):
    b = lax.bitcast_convert_type(x, I32)
    return b ^ ((b >> 31) & INT_MAG)


def _key2f(k):
    return lax.bitcast_convert_type(k ^ ((k >> 31) & INT_MAG), F32)


def _select_topk(count, rewrite, top, n_valid, mx, mn, last_pos, index_steps):
    topf = float(top)

    small = n_valid <= top
    lo0 = _f2key(mn)
    hi0 = _f2key(mx) + 1
    conv0 = jnp.logical_and(jnp.logical_not(small), lo0 + 1 >= hi0)
    done0 = jnp.logical_or(small, conv0)
    thr0 = mn
    state0 = (lo0, hi0, jnp.zeros(mn.shape, F32), thr0, done0.astype(I32), conv0.astype(I32),
              jnp.int32(0), jnp.sum(1.0 - done0.astype(F32)))

    def cond(st):
        return st[-1] > 0.0

    def body(st):
        lo, hi, ghi, thr, done_i, tie_i, it, _ = st
        done = done_i > 0
        lof, hif = _key2f(lo), _key2f(hi)
        midf = 0.5 * lof + 0.5 * hif
        use_f = jnp.logical_and(it < FLOAT_BISECT_STEPS, jnp.logical_and(midf > lof, midf < hif))
        mid = jnp.where(use_f, _f2key(midf), (lo & hi) + ((lo ^ hi) >> 1))
        tmid = _key2f(mid)
        g = count(lambda x, kpos: x >= tmid)
        live = jnp.logical_not(done)
        is_eq = jnp.logical_and(live, g == topf)
        up = jnp.logical_and(live, g > topf)
        down = jnp.logical_and(live, g < topf)
        lo = jnp.where(up, mid, lo)
        hi = jnp.where(down, mid, hi)
        ghi = jnp.where(down, g, ghi)
        thr = jnp.where(is_eq, tmid, thr)
        done = jnp.logical_or(done, is_eq)
        conv = jnp.logical_and(jnp.logical_not(done), lo + 1 >= hi)
        thr = jnp.where(conv, _key2f(lo), thr)
        done = jnp.logical_or(done, conv)
        tie_i = jnp.where(conv, 1, tie_i)
        return (lo, hi, ghi, thr, done.astype(I32), tie_i, it + 1, jnp.sum(1.0 - done.astype(F32)))

    _, _, ghi, thr, _, tie_i, _, _ = lax.while_loop(cond, body, state0)

    tie = tie_i > 0
    n_tie = jnp.sum(tie_i.astype(F32))

    @pl.when(n_tie > 0.0)
    def _():
        need = topf - ghi
        jlo0 = jnp.full(mn.shape, -1, I32)
        jhi0 = jnp.full(mn.shape, 0, I32) + last_pos

        def jbody(_, carry):
            jlo, jhi = carry
            jm = (jlo + jhi) >> 1
            cnt = count(lambda x, kpos: jnp.logical_and(x == thr, kpos <= jm))
            ok = cnt >= need
            return jnp.where(ok, jlo, jm), jnp.where(ok, jm, jhi)

        _, jcut = lax.fori_loop(0, index_steps, jbody, (jlo0, jhi0))
        rewrite(lambda x, kpos: jnp.where(
            jnp.logical_and(jnp.logical_and(tie, x == thr), kpos > jcut), NEG_INF, x))

    rewrite(lambda x, kpos: jnp.where(x >= thr, 0.0, NEG_INF))


def _index_steps(n_positions):
    return max(1, (n_positions - 1).bit_length()) + 1


def _attn_prompt_kernel(qit_ref, wit_ref, kitok_ref, qt_ref, ktok_ref, vt_ref, o_ref,
                        sc_ref, qis_ref, qs_ref, m_ref, l_ref, acc_ref, s_ref, *, qb, kc, top):
    i = pl.program_id(1)
    nkc = ((i + 1) * qb + kc - 1) // kc

    for h in range(IDX_HEADS):
        qis_ref[:, h * qb:(h + 1) * qb] = qit_ref[0, h * IDX_DIM:(h + 1) * IDX_DIM, :]
    feat = lax.broadcasted_iota(I32, (LANES, qb), 0)
    for hp in range(NPAIR):
        qp = qt_ref[0, hp * LANES:(hp + 1) * LANES, :].astype(F32)
        qs_ref[hp] = jnp.concatenate([jnp.where(feat < HEAD_DIM, qp, 0.0),
                                      jnp.where(feat >= HEAD_DIM, qp, 0.0)], axis=1).astype(BF16)

    wit = wit_ref[0]
    wrows = [wit[h:h + 1, :] for h in range(IDX_HEADS)]
    qpos = i * qb + lax.broadcasted_iota(I32, (1, qb), 1)
    krow = lax.broadcasted_iota(I32, (kc, 1), 0)

    def kpos_of(c):
        return c * kc + krow

    def score(c, carry):
        mxp, mnp = carry
        half = IDX_HEADS * qb // 2
        kic = kitok_ref[0, c]
        d = jnp.concatenate([_dot(kic, qis_ref[:, 0:half]), _dot(kic, qis_ref[:, half:2 * half])], axis=1)
        acc = _tree(lambda a, b: a + b,
                    [wrows[h] * jnp.maximum(d[:, h * qb:(h + 1) * qb], 0.0) for h in range(IDX_HEADS)])
        valid = kpos_of(c) <= qpos
        x = jnp.where(valid, acc, NEG_INF)
        sc_ref[c] = x
        mxp = jnp.maximum(mxp, _fold_rows(x, jnp.maximum))
        mnp = jnp.minimum(mnp, _fold_rows(jnp.where(valid, acc, -NEG_INF), jnp.minimum))
        return mxp, mnp

    mxp, mnp = lax.fori_loop(0, nkc, score, (jnp.full((SUBLANES, qb), NEG_INF, F32),
                                             jnp.full((SUBLANES, qb), -NEG_INF, F32)))
    mx = jnp.max(mxp, axis=0, keepdims=True)
    mn = jnp.min(mnp, axis=0, keepdims=True)

    def count(pred):
        def body(c, acc):
            return acc + _fold_rows(jnp.where(pred(sc_ref[c], kpos_of(c)), 1.0, 0.0), lambda a, b: a + b)
        acc = lax.fori_loop(0, nkc, body, jnp.zeros((SUBLANES, qb), F32))
        return jnp.sum(acc, axis=0, keepdims=True)

    def rewrite(fn):
        def body(c, _):
            sc_ref[c] = fn(sc_ref[c], kpos_of(c))
            return 0
        lax.fori_loop(0, nkc, body, 0)

    _select_topk(count, rewrite, top, qpos + 1, mx, mn, nkc * kc - 1, _index_steps(sc_ref.shape[0] * kc))

    m_ref[...] = jnp.full(m_ref.shape, M_INIT, F32)
    l_ref[...] = jnp.zeros(l_ref.shape, F32)
    acc_ref[...] = jnp.zeros(acc_ref.shape, F32)

    def stage_a(c, slot):
        bias = sc_ref[c]
        bias2 = jnp.concatenate([bias, bias], axis=1)
        cmax = []
        for hp in range(NPAIR):
            s = _dot(ktok_ref[0, c, :, hp * LANES:(hp + 1) * LANES], qs_ref[hp]) + bias2
            s_ref[slot, hp] = s
            cmax.append(jnp.max(_fold_rows(s, jnp.maximum), axis=0, keepdims=True))
        return cmax

    def stage_b(c, slot, cmax):
        for hp in range(NPAIR):
            m = m_ref[hp]
            m_new = jnp.maximum(m, cmax[hp])
            alpha = jnp.exp2(m - m_new)
            p = jnp.exp2(s_ref[slot, hp] - m_new)
            l_ref[hp] = alpha * l_ref[hp] + jnp.sum(_fold_rows(p, lambda a, b: a + b), axis=0, keepdims=True)
            acc_ref[hp] = alpha * acc_ref[hp] + _dot(vt_ref[0, c, hp * LANES:(hp + 1) * LANES, :], p.astype(BF16))
            m_ref[hp] = m_new

    def two_chunks(j, cmax0):
        cmax1 = stage_a(2 * j + 1, 1)
        stage_b(2 * j, 0, cmax0)
        cmax0 = stage_a(2 * j + 2, 0)
        stage_b(2 * j + 1, 1, cmax1)
        return cmax0

    npairs = (nkc - 1) // 2
    cmax0 = lax.fori_loop(0, npairs, two_chunks, stage_a(0, 0))
    last = 2 * npairs

    @pl.when(last + 1 < nkc)
    def _():
        cmax1 = stage_a(last + 1, 1)
        stage_b(last, 0, cmax0)
        stage_b(last + 1, 1, cmax1)

    @pl.when(last + 1 >= nkc)
    def _():
        stage_b(last, 0, cmax0)

    for hp in range(NPAIR):
        out = acc_ref[hp] / l_ref[hp]
        out = jnp.where(feat < HEAD_DIM, out[:, 0:qb], out[:, qb:2 * qb])
        o_ref[0, :, hp * LANES:(hp + 1) * LANES] = out.T


def _attn_prompt(qit, wit, kitok, qt, ktok, vtc, qb):
    b, _, t = qt.shape
    nc, kc = ktok.shape[1], ktok.shape[2]
    top = min(TOPK_MAX, t // 4)
    assert t % qb == 0 and nc * kc == t and kc % qb == 0 and qb % LANES == 0
    qblk = lambda n: pl.BlockSpec((1, n, qb), lambda bi, i: (bi, 0, i))
    whole = lambda a: pl.BlockSpec((1,) + a.shape[1:], lambda bi, i: (bi, 0, 0, 0))
    return pl.pallas_call(
        functools.partial(_attn_prompt_kernel, qb=qb, kc=kc, top=top),
        grid=(b, t // qb),
        in_specs=[qblk(IDX_WIDTH), qblk(IDX_HEADS), whole(kitok), qblk(ATT_WIDTH), whole(ktok), whole(vtc)],
        out_specs=pl.BlockSpec((1, qb, ATT_WIDTH), lambda bi, i: (bi, i, 0)),
        out_shape=jax.ShapeDtypeStruct((b, t, ATT_WIDTH), F32),
        scratch_shapes=[pltpu.VMEM((nc, kc, qb), F32),
                        pltpu.VMEM((IDX_DIM, IDX_HEADS * qb), BF16),
                        pltpu.VMEM((NPAIR, LANES, 2 * qb), BF16),
                        pltpu.VMEM((NPAIR, 1, 2 * qb), F32),
                        pltpu.VMEM((NPAIR, 1, 2 * qb), F32),
                        pltpu.VMEM((NPAIR, LANES, 2 * qb), F32),
                        pltpu.VMEM((2, NPAIR, kc, 2 * qb), F32)],
        compiler_params=_cparams(("arbitrary", "arbitrary")),
        name="attn_prompt",
    )(qit, wit, kitok, qt, ktok, vtc)


def _split_pair(qp, lane):
    return jnp.concatenate([jnp.where(lane < HEAD_DIM, qp, 0.0), jnp.where(lane >= HEAD_DIM, qp, 0.0)], axis=0)


def _merge_pair(x, lane, rows):
    return jnp.where(lane < HEAD_DIM, x[0:rows], x[rows:2 * rows])


def _indexer_scores(d, wfull, rows):
    return _tree(lambda a, b: a + b,
                 [wfull[h] * jnp.maximum(d[h * rows:(h + 1) * rows], 0.0) for h in range(IDX_HEADS)])


def _index_sample_kernel(pt_ref, qi_ref, wi_ref, kin_ref, cki_hbm, bias_ref,
                         kibuf_ref, qis_ref, knew_ref, sem, *, tq, npages, top, unroll):
    b = pl.program_id(0)
    past = npages * PAGE
    nkc = npages + 1

    def page_copy(p):
        return pltpu.make_async_copy(cki_hbm.at[pt_ref[b, p]], kibuf_ref.at[p], sem)

    def start(p, _):
        page_copy(p).start()
        return 0
    lax.fori_loop(0, npages, start, 0)

    knew_ref[...] = jnp.zeros(knew_ref.shape, F32)
    knew_ref[0:tq, :] = kin_ref[0]
    qi = qi_ref[0].astype(F32)
    for h in range(IDX_HEADS):
        qis_ref[h * tq:(h + 1) * tq, :] = qi[:, h * IDX_DIM:(h + 1) * IDX_DIM]
    wi = wi_ref[0]
    wfull = [jnp.broadcast_to(wi[:, h:h + 1], (tq, PAGE)) for h in range(IDX_HEADS)]
    qpos = past + lax.broadcasted_iota(I32, (tq, 1), 0)
    kpos = (lax.broadcasted_iota(I32, (nkc, 1, PAGE), 0) * PAGE + lax.broadcasted_iota(I32, (nkc, 1, PAGE), 2))

    def wait(p, _):
        page_copy(p).wait()
        return 0
    lax.fori_loop(0, npages, wait, 0)

    qis = qis_ref[...].astype(BF16)

    def score(cc, carry):
        mxp, mnp = carry
        for j in range(unroll):
            c = cc * unroll + j
            acc = _indexer_scores(_dot(qis, kibuf_ref[c].astype(BF16)), wfull, tq)
            bias_ref[0, c] = acc
            mxp = jnp.maximum(mxp, acc)
            mnp = jnp.minimum(mnp, acc)
        return mxp, mnp

    mxp, mnp = lax.fori_loop(0, npages // unroll, score,
                             (jnp.full((tq, PAGE), NEG_INF, F32), jnp.full((tq, PAGE), -NEG_INF, F32)))
    acc = _indexer_scores(_dot_nt(qis, knew_ref[...].astype(BF16)), wfull, tq)
    valid = kpos[npages] <= qpos
    bias_ref[0, npages] = jnp.where(valid, acc, NEG_INF)
    mx = jnp.max(jnp.maximum(mxp, jnp.where(valid, acc, NEG_INF)), axis=-1, keepdims=True)
    mn = jnp.min(jnp.minimum(mnp, jnp.where(valid, acc, -NEG_INF)), axis=-1, keepdims=True)

    def count(pred):
        ind = jnp.where(pred(bias_ref[0], kpos), 1.0, 0.0)
        return jnp.sum(jnp.sum(ind, axis=0), axis=-1, keepdims=True)

    def rewrite(fn):
        bias_ref[0] = fn(bias_ref[0], kpos)

    _select_topk(count, rewrite, top, qpos + 1, mx, mn, nkc * PAGE - 1, _index_steps(nkc * PAGE))


def _index_sample(page_table, qi, wi, ki_new, cki_t):
    b, tq, _ = qi.shape
    npages = page_table.shape[1]
    top = min(TOPK_MAX, (npages * PAGE + tq) // 4)
    nkc = npages + 1
    unroll = 8 if npages % 8 == 0 else 1
    assert tq <= PAGE
    blk = lambda n: pl.BlockSpec((1, tq, n), lambda bi, pt: (bi, 0, 0))
    grid_spec = pltpu.PrefetchScalarGridSpec(
        num_scalar_prefetch=1,
        grid=(b,),
        in_specs=[blk(IDX_WIDTH), blk(IDX_HEADS), blk(IDX_DIM), pl.BlockSpec(memory_space=pl.ANY)],
        out_specs=pl.BlockSpec((1, nkc, tq, PAGE), lambda bi, pt: (bi, 0, 0, 0)),
        scratch_shapes=[pltpu.VMEM((npages, IDX_DIM, PAGE), F32),
                        pltpu.VMEM((IDX_HEADS * tq, IDX_DIM), F32),
                        pltpu.VMEM((PAGE, IDX_DIM), F32),
                        pltpu.SemaphoreType.DMA(())],
    )
    return pl.pallas_call(
        functools.partial(_index_sample_kernel, tq=tq, npages=npages, top=top, unroll=unroll),
        grid_spec=grid_spec,
        out_shape=jax.ShapeDtypeStruct((b, nkc, tq, PAGE), F32),
        compiler_params=_cparams(("arbitrary",)),
        name="index_sample",
    )(page_table, qi, wi, ki_new, cki_t)


def _attn_sample_kernel(pt_ref, q_ref, kn_ref, vn_ref, bias_ref, ck_hbm, cv_hbm, o_ref,
                        kbuf_ref, vbuf_ref, qs_ref, knew_ref, vnew_ref, m_ref, l_ref, acc_ref, sems,
                        *, tq, gp, npages, nb):
    b = pl.program_id(0)
    g = pl.program_id(1)
    ng = npages // gp
    step = b * ng + g
    slot = step % 2

    def group_copies(bb, gg, sl):
        out = []
        for j in range(gp):
            page = pt_ref[bb, gg * gp + j]
            out.append(pltpu.make_async_copy(ck_hbm.at[page], kbuf_ref.at[sl, j], sems.at[0, sl]))
            out.append(pltpu.make_async_copy(cv_hbm.at[page], vbuf_ref.at[sl, j], sems.at[1, sl]))
        return out

    @pl.when(step == 0)
    def _():
        for cp in group_copies(b, g, slot):
            cp.start()

    @pl.when(step + 1 < nb * ng)
    def _():
        nxt = step + 1
        for cp in group_copies(nxt // ng, nxt % ng, 1 - slot):
            cp.start()

    lane = lax.broadcasted_iota(I32, (tq, LANES), 1)

    @pl.when(g == 0)
    def _():
        q = q_ref[0].astype(F32)
        for hp in range(NPAIR):
            qs_ref[hp] = _split_pair(q[:, hp * LANES:(hp + 1) * LANES], lane)
        m_ref[...] = jnp.full(m_ref.shape, M_INIT, F32)
        l_ref[...] = jnp.zeros(l_ref.shape, F32)
        acc_ref[...] = jnp.zeros(acc_ref.shape, F32)

    for cp in group_copies(b, g, slot):
        cp.wait()

    def bias2(c):
        bias = bias_ref[0, c]
        return jnp.concatenate([bias, bias], axis=0)

    def update(hp, s_list, pv_of):
        m = m_ref[hp]
        m_new = m
        for s in s_list:
            m_new = jnp.maximum(m_new, jnp.max(s, axis=-1, keepdims=True))
        alpha = jnp.exp(m - m_new)
        l = alpha * l_ref[hp]
        acc = alpha * acc_ref[hp]
        for j, s in enumerate(s_list):
            p = jnp.exp(s - m_new)
            l = l + jnp.sum(p, axis=-1, keepdims=True)
            acc = acc + pv_of(j, p.astype(BF16))
        m_ref[hp] = m_new
        l_ref[hp] = l
        acc_ref[hp] = acc

    for hp in range(NPAIR):
        qs = qs_ref[hp].astype(BF16)
        pair = slice(hp * LANES, (hp + 1) * LANES)
        s_list = [_dot(qs, kbuf_ref[slot, j, pair, :].astype(BF16)) + bias2(g * gp + j) for j in range(gp)]
        update(hp, s_list, lambda j, p, pair=pair: _dot_nt(p, vbuf_ref[slot, j, pair, :].astype(BF16)))

    @pl.when(g == ng - 1)
    def _():
        knew_ref[...] = jnp.zeros(knew_ref.shape, F32)
        vnew_ref[...] = jnp.zeros(vnew_ref.shape, F32)
        knew_ref[0:tq, :] = kn_ref[0]
        vnew_ref[0:tq, :] = vn_ref[0]
        for hp in range(NPAIR):
            qs = qs_ref[hp].astype(BF16)
            pair = slice(hp * LANES, (hp + 1) * LANES)
            s = _dot_nt(qs, knew_ref[:, pair].astype(BF16)) + bias2(npages)
            update(hp, [s], lambda j, p, pair=pair: _dot(p, vnew_ref[:, pair].astype(BF16)))
            o_ref[0, :, pair] = _merge_pair(acc_ref[hp] / l_ref[hp], lane, tq)


def _attn_sample(page_table, q, k_new, v_new, bias, ck_t, cv_t, gp):
    b, tq, _ = q.shape
    npages = page_table.shape[1]
    assert npages % gp == 0 and bias.shape[1] == npages + 1 and tq <= PAGE
    blk = lambda n: pl.BlockSpec((1, tq, n), lambda bi, gi, pt: (bi, 0, 0))
    grid_spec = pltpu.PrefetchScalarGridSpec(
        num_scalar_prefetch=1,
        grid=(b, npages // gp),
        in_specs=[blk(ATT_WIDTH), blk(ATT_WIDTH), blk(ATT_WIDTH),
                  pl.BlockSpec((1, npages + 1, tq, PAGE), lambda bi, gi, pt: (bi, 0, 0, 0)),
                  pl.BlockSpec(memory_space=pl.ANY), pl.BlockSpec(memory_space=pl.ANY)],
        out_specs=blk(ATT_WIDTH),
        scratch_shapes=[pltpu.VMEM((2, gp, ATT_WIDTH, PAGE), F32),
                        pltpu.VMEM((2, gp, ATT_WIDTH, PAGE), F32),
                        pltpu.VMEM((NPAIR, 2 * tq, LANES), F32),
                        pltpu.VMEM((PAGE, ATT_WIDTH), F32),
                        pltpu.VMEM((PAGE, ATT_WIDTH), F32),
                        pltpu.VMEM((NPAIR, 2 * tq, 1), F32),
                        pltpu.VMEM((NPAIR, 2 * tq, 1), F32),
                        pltpu.VMEM((NPAIR, 2 * tq, LANES), F32),
                        pltpu.SemaphoreType.DMA((2, 2))],
    )
    return pl.pallas_call(
        functools.partial(_attn_sample_kernel, tq=tq, gp=gp, npages=npages, nb=b),
        grid_spec=grid_spec,
        out_shape=jax.ShapeDtypeStruct((b, tq, ATT_WIDTH), F32),
        compiler_params=_cparams(("arbitrary", "arbitrary")),
        name="attn_sample",
    )(page_table, q, k_new, v_new, bias, ck_t, cv_t)


def _cumsum_rows(x):
    n = x.shape[0]
    row = lax.broadcasted_iota(I32, x.shape, 0)
    sh = 1
    while sh < n:
        x = x + jnp.where(row >= sh, pltpu.roll(x, sh, 0), 0.0)
        sh *= 2
    return x


def _hgrn_kernel(hq_ref, lf_ref, hk_ref, hv_ref, hg_ref, s0_ref, ng_ref, o_ref, sout_ref,
                 st_ref, *, tb, c, sb):
    t = pl.program_id(2)
    nt = pl.num_programs(2)

    @pl.when(t == 0)
    def _():
        st_ref[...] = s0_ref[0, 0].T

    row_sb = lax.broadcasted_iota(I32, (sb, 1), 0)
    row_c = lax.broadcasted_iota(I32, (c, 1), 0)
    nsb = c // sb

    for j in range(tb // c):
        rows = slice(j * c, (j + 1) * c)
        lf = lf_ref[0, rows, :]
        q = hq_ref[0, rows, :]
        k = hk_ref[0, rows, :]
        v = hv_ref[0, rows, :]
        cum = _cumsum_rows(lf)
        last = cum[c - 1:c, :]
        st = st_ref[...]
        o = _dot_nt((q * jnp.exp(cum)).astype(BF16), st.astype(BF16))

        vb = v.astype(BF16)
        o_parts = []
        a_rows = []
        for i in range(nsb):
            r = slice(i * sb, (i + 1) * sb)
            cum_i, q_i = cum[r], q[r]
            od = jnp.zeros((sb, HG_DV), F32)
            for s in range(sb):
                sr = i * sb + s
                e = jnp.exp(jnp.where(row_sb >= s, cum_i - cum[sr:sr + 1, :], NEG_INF))
                a_col = jnp.sum(e * (q_i * k[sr:sr + 1, :]), axis=-1, keepdims=True)
                od = od + a_col * v[sr:sr + 1, :]
            o_parts.append(od)
            if i > 0:
                c0 = cum[i * sb:i * sb + 1, :] - lf[i * sb:i * sb + 1, :]
                qs = (q_i * jnp.exp(cum_i - c0)).astype(BF16)
                ks = (k * jnp.exp(jnp.where(row_c < i * sb, c0 - cum, NEG_INF))).astype(BF16)
                a_rows.append(_dot_nt(qs, ks))
            elif nsb > 1:
                a_rows.append(jnp.zeros((sb, c), F32))
        o = o + jnp.concatenate(o_parts, axis=0)
        if nsb > 1:
            o = o + _dot(jnp.concatenate(a_rows, axis=0).astype(BF16), vb)

        kd = (k * jnp.exp(last - cum)).astype(BF16)
        st_ref[...] = st * jnp.exp(last) + _dot_tn(vb, kd)

        ms = jnp.mean(o * o, axis=-1, keepdims=True)
        hg = hg_ref[0, rows, :]
        o_ref[0, rows, :] = (((o * lax.rsqrt(ms + RMS_EPS)) * ng_ref[...]) * (hg * _sigmoid(hg))).astype(BF16)

    @pl.when(t == nt - 1)
    def _():
        sout_ref[0, 0] = st_ref[...].T


def _hgrn(hq, lf, hk, hv, hg, s0, norm_g, tb, c):
    b, t, _ = hq.shape
    sb = 16
    assert t % tb == 0 and tb % c == 0 and c % sb == 0
    blk = pl.BlockSpec((1, tb, HG_DK), lambda bi, h, ti: (bi, ti, h))
    sblk = pl.BlockSpec((1, 1, HG_DK, HG_DV), lambda bi, h, ti: (bi, h, 0, 0))
    return pl.pallas_call(
        functools.partial(_hgrn_kernel, tb=tb, c=c, sb=sb),
        grid=(b, HG_HEADS, t // tb),
        in_specs=[blk, blk, blk, blk, blk, sblk, pl.BlockSpec((1, HG_DV), lambda bi, h, ti: (0, 0))],
        out_specs=[blk, sblk],
        out_shape=[jax.ShapeDtypeStruct((b, t, HG_WIDTH), BF16),
                   jax.ShapeDtypeStruct((b, HG_HEADS, HG_DK, HG_DV), F32)],
        scratch_shapes=[pltpu.VMEM((HG_DV, HG_DK), F32)],
        compiler_params=_cparams(("arbitrary", "arbitrary", "arbitrary")),
        name="hgrn",
    )(hq, lf, hk, hv, hg, s0, norm_g)


def _merge_kernel(x_ref, att_ref, ga_ref, hb_ref, ma_ref, mb_ref, p_ref,
                  wa_ref, wb_ref, wo_ref, wpg_ref, wpp_ref, gp_ref, o_ref):
    ga = ga_ref[...]
    ya = _dot((att_ref[...] * (ga * _sigmoid(ga))).astype(BF16), wa_ref[...])
    yb = _dot(hb_ref[...], wb_ref[...])
    mixed = _sigmoid(ma_ref[...]) * ya + _sigmoid(mb_ref[...]) * yb
    u = _dot(mixed.astype(BF16), wo_ref[...])
    ms = jnp.mean(u * u, axis=-1, keepdims=True)
    h1 = x_ref[...] + (u * lax.rsqrt(ms + RMS_EPS)) * gp_ref[...]
    gate = _sigmoid(_dot(h1.astype(BF16), wpg_ref[...]))
    o_ref[...] = h1 + gate * _dot(p_ref[...].astype(BF16), wpp_ref[...])


def _merge(x2, att, ga, hb, ma, mb, p2, w_a, w_b, w_o, w_pg, w_pp, g_post, tm):
    m, d = x2.shape
    assert m % tm == 0
    row = lambda a: pl.BlockSpec((tm, a.shape[1]), lambda i: (i, 0))
    full = lambda a: pl.BlockSpec(a.shape, lambda i: (0, 0))
    acts = (x2, att, ga, hb, ma, mb, p2)
    wts = (w_a, w_b, w_o, w_pg, w_pp, g_post)
    return pl.pallas_call(
        _merge_kernel,
        grid=(m // tm,),
        in_specs=[row(a) for a in acts] + [full(w) for w in wts],
        out_specs=pl.BlockSpec((tm, d), lambda i: (i, 0)),
        out_shape=jax.ShapeDtypeStruct((m, d), F32),
        compiler_params=_cparams(("arbitrary",)),
        name="merge",
    )(*acts, *wts)


def _pick_rows(m, pref):
    return pref if m % pref == 0 else m


def kernel(x_prompt, x_sample, p_prompt, p_sample, cache_k, cache_v, cache_kidx, state_hgrn, page_table,
           g_pre, g_post, w_in, hgrn_lb_logits, hgrn_norm_g, w_branch_a, w_branch_b, w_out,
           w_ple_gate, w_ple_proj):
    depth = w_in.shape[0]
    bp, tp, d = x_prompt.shape
    bs, ts, _ = x_sample.shape
    nphys = cache_k.shape[1]
    hp, hs = x_prompt, x_sample
    outs = [[] for _ in range(8)]
    w = ATT_WIDTH
    a_end = 4 * w + IDX_WIDTH
    b_end = a_end + IDX_DIM + IDX_HEADS

    for i in range(depth):
        wi_ = w_in[i].astype(BF16)
        wa, wb, wc = wi_[:, :a_end], wi_[:, a_end:b_end], wi_[:, b_end:]
        wt = jnp.concatenate([wi_[:, 0:3 * w], wi_[:, 4 * w:b_end]], axis=1).T
        wtok = jnp.concatenate([wi_[:, w:2 * w], wi_[:, 3 * w:4 * w], wc], axis=1)
        g_pre_i = g_pre[i].reshape(1, d)
        wts = (w_branch_a[i].astype(BF16), w_branch_b[i].astype(BF16), w_out[i].astype(BF16),
               w_ple_gate[i].astype(BF16), w_ple_proj[i].astype(BF16), g_post[i].reshape(1, d))
        norm_g = hgrn_norm_g[i].reshape(1, HG_DV)

        mp = bp * tp
        kc = _pick_rows(tp, 256)
        (qt, kt, ktok, vt, vtc, ga, qit, kit, kitok, wit, hq, lf, hk, hv, hg, ma, mb) = _inproj_prompt(
            hp.reshape(mp, d), g_pre_i, hgrn_lb_logits, wt, wtok, wb, i, kc, bp)
        r3 = lambda a, b_=bp, t_=tp: a.reshape(b_, t_, a.shape[-1])
        att = _attn_prompt(qit, wit, kitok, qt, ktok, vtc, qb=128)
        s0 = jnp.zeros((bp, HG_HEADS, HG_DK, HG_DV), F32)
        hb, s_p = _hgrn(r3(hq), r3(lf), r3(hk), r3(hv), r3(hg), s0, norm_g, tb=_pick_rows(tp, 256), c=64)
        y = _merge(hp.reshape(mp, d), att.reshape(mp, w), ga, hb.reshape(mp, HG_WIDTH), ma, mb,
                   p_prompt[i].reshape(mp, -1), *wts, _pick_rows(mp, 256))
        hp = y.reshape(bp, tp, d)
        outs[0].append(kt.reshape(bp, N_HEADS, HEAD_DIM, tp).transpose(0, 3, 1, 2))
        outs[1].append(vt.reshape(bp, N_HEADS, HEAD_DIM, tp).transpose(0, 3, 1, 2))
        outs[2].append(kit.transpose(0, 2, 1))
        outs[3].append(s_p.astype(state_hgrn.dtype))

        ms = bs * ts
        (q, k, v, ga, qi, ki, wi, hq, lf, hk, hv, hg, ma, mb) = _inproj_sample(
            hs.reshape(ms, d), g_pre_i, hgrn_lb_logits, wa, wb, wc, i, _pick_rows(ms, 256))
        r3 = lambda a, b_=bs, t_=ts: a.reshape(b_, t_, a.shape[-1])
        cki_t = cache_kidx[i].transpose(0, 2, 1)
        ck_t = cache_k[i].transpose(0, 2, 3, 1).reshape(nphys, w, PAGE)
        cv_t = cache_v[i].transpose(0, 2, 3, 1).reshape(nphys, w, PAGE)
        bias = _index_sample(page_table, r3(qi), r3(wi), r3(ki), cki_t)
        att = _attn_sample(page_table, r3(q), r3(k), r3(v), bias, ck_t, cv_t, gp=8)
        tpad = -ts % 16
        pad = lambda a: jnp.pad(r3(a), ((0, 0), (0, tpad), (0, 0)))
        hb, s_s = _hgrn(pad(hq), pad(lf), pad(hk), pad(hv), pad(hg), state_hgrn[i], norm_g,
                        tb=ts + tpad, c=ts + tpad)
        y = _merge(hs.reshape(ms, d), att.reshape(ms, w), ga, hb[:, :ts].reshape(ms, HG_WIDTH), ma, mb,
                   p_sample[i].reshape(ms, -1), *wts, _pick_rows(ms, 256))
        hs = y.reshape(bs, ts, d)
        outs[4].append(k.reshape(bs, ts, N_HEADS, HEAD_DIM))
        outs[5].append(v.reshape(bs, ts, N_HEADS, HEAD_DIM))
        outs[6].append(ki.reshape(bs, ts, IDX_DIM))
        outs[7].append(s_s.astype(state_hgrn.dtype))

    st = [jnp.stack(o) for o in outs]
    return (hp, hs, st[0], st[1], st[2], st[3], st[4], st[5], st[6], st[7])
```

```python
import functools

import jax
import jax.numpy as jnp
from jax import lax
from jax.experimental import pallas as pl
from jax.experimental.pallas import tpu as pltpu

F32 = jnp.float32
BF16 = jnp.bfloat16
I32 = jnp.int32

N_HEADS = 8
HEAD_DIM = 64
ATT_WIDTH = N_HEADS * HEAD_DIM
IDX_HEADS = 8
IDX_DIM = 64
IDX_WIDTH = IDX_HEADS * IDX_DIM
TOPK_MAX = 256
HG_HEADS = 4
HG_DK = 128
HG_DV = 128
HG_WIDTH = HG_HEADS * HG_DV
PAGE = 128
RMS_EPS = 1e-6

LANES = 128
SUBLANES = 8
BF16_ROWS = 16
VMEM_LIMIT = 56 * 1024 * 1024
NEG_INF = float("-inf")
M_INIT = -1e30
INT_MAG = 0x7FFFFFFF
NPAIR = N_HEADS // 2
LOG2E = 1.4426950408889634
FLOAT_BISECT_STEPS = 12


def _cparams(sem):
    return pltpu.CompilerParams(dimension_semantics=sem, vmem_limit_bytes=VMEM_LIMIT)


def _sigmoid(x):
    return 1.0 / (1.0 + jnp.exp(-x))


def _dot(a, b):
    return jnp.dot(a, b, preferred_element_type=F32)


def _dot_nt(a, b):
    return lax.dot_general(a, b, (((1,), (1,)), ((), ())), preferred_element_type=F32)


def _dot_tn(a, b):
    return lax.dot_general(a, b, (((0,), (0,)), ((), ())), preferred_element_type=F32)


def _tree(op, xs):
    xs = list(xs)
    while len(xs) > 1:
        xs = [op(xs[i], xs[i + 1]) if i + 1 < len(xs) else xs[i] for i in range(0, len(xs), 2)]
    return xs[0]


def _fold_rows(x, op):
    return _tree(op, [x[r:r + SUBLANES] for r in range(0, x.shape[0], SUBLANES)])


def _two_stage(n, stage_a, stage_b, carry):
    def two(j, st):
        a0, carry = st
        a1 = stage_a(2 * j + 1, 1)
        carry = stage_b(2 * j, 0, a0, carry)
        a0 = stage_a(2 * j + 2, 0)
        carry = stage_b(2 * j + 1, 1, a1, carry)
        return a0, carry

    npairs = (n - 1) // 2
    a0, carry = lax.fori_loop(0, npairs, two, (stage_a(0, 0), carry))
    last = 2 * npairs

    def tail2():
        a1 = stage_a(last + 1, 1)
        return stage_b(last + 1, 1, a1, stage_b(last, 0, a0, carry))

    return lax.cond(last + 1 < n, tail2, lambda: stage_b(last, 0, a0, carry))


def _chunk_loop2(n, body, carry):
    carry = lax.fori_loop(0, n // 2, lambda j, cr: body(2 * j + 1, body(2 * j, cr)), carry)
    return lax.cond(n % 2 == 1, lambda: body(n - 1, carry), lambda: carry)


def _normed(x_ref, g_ref):
    x = x_ref[...]
    var = jnp.mean(x * x, axis=-1, keepdims=True)
    return ((x * lax.rsqrt(var + RMS_EPS)) * g_ref[...]).astype(BF16)


def _hgrn_and_merge_gates(xn, w_ref, off, lbl_ref, layer, hq_ref, lf_ref, hk_ref, hv_ref, hg_ref, ma_ref, mb_ref):
    def proj(lo, hi):
        return _dot(xn, w_ref[:, off + lo:off + hi])

    lg = lbl_ref[...]
    e = jnp.exp(lg - jnp.max(lg, axis=0, keepdims=True))
    sm = e / jnp.sum(e, axis=0, keepdims=True)
    lb = jnp.sum(sm[:layer + 1], axis=0, keepdims=True)
    one_m = 1.0 - lb

    hw = HG_WIDTH
    hq = proj(0, hw)
    hq_ref[...] = hq * _sigmoid(hq)
    zf = proj(hw, 2 * hw)
    lf_ref[...] = jnp.log(lb + one_m * _sigmoid(zf))
    hk_ref[...] = one_m * _sigmoid(-zf)
    hv_ref[...] = proj(2 * hw, 3 * hw)
    hg_ref[...] = proj(3 * hw, 4 * hw)
    d = ma_ref.shape[-1]
    ma_ref[...] = proj(4 * hw, 4 * hw + d)
    mb_ref[...] = proj(4 * hw + d, 4 * hw + 2 * d)


def _inproj_sample_kernel(x_ref, g_ref, lbl_ref, wa_ref, wb_ref, wc_ref,
                          q_ref, k_ref, v_ref, ga_ref, qi_ref, ki_ref, wi_ref, *tail, layer):
    xn = _normed(x_ref, g_ref)
    w = ATT_WIDTH
    q_ref[...] = (_dot(xn, wa_ref[:, 0:w]) * (HEAD_DIM ** -0.5)).astype(BF16)
    k_ref[...] = _dot(xn, wa_ref[:, w:2 * w])
    v_ref[...] = _dot(xn, wa_ref[:, 2 * w:3 * w])
    ga_ref[...] = _dot(xn, wa_ref[:, 3 * w:4 * w])
    qi_ref[...] = (_dot(xn, wa_ref[:, 4 * w:4 * w + IDX_WIDTH]) * (IDX_DIM ** -0.5)).astype(BF16)
    kiwi = _dot(xn, wb_ref[...])
    ki_ref[...] = kiwi[:, :IDX_DIM]
    wi_ref[...] = kiwi[:, IDX_DIM:IDX_DIM + IDX_HEADS] * (IDX_HEADS ** -0.5)
    _hgrn_and_merge_gates(xn, wc_ref, 0, lbl_ref, layer, *tail)


def _inproj_prompt_kernel(x_ref, g_ref, lbl_ref, wt_ref, wtok_ref, wb_ref,
                          qt_ref, kt_ref, ktok_ref, vt_ref, vtc_ref, ga_ref, qit_ref, kit_ref, kitok_ref, wit_ref,
                          *tail, layer):
    xn = _normed(x_ref, g_ref)
    w = ATT_WIDTH

    def tproj(lo, hi):
        return _dot_nt(wt_ref[lo:hi, :], xn)

    qt_ref[0] = (tproj(0, w) * (HEAD_DIM ** -0.5 * LOG2E)).astype(BF16)
    kt_ref[0] = tproj(w, 2 * w)
    vt = tproj(2 * w, 3 * w)
    vt_ref[0] = vt
    vtc_ref[0, 0] = vt.astype(BF16)
    o = 3 * w
    qit_ref[0] = (tproj(o, o + IDX_WIDTH) * (IDX_DIM ** -0.5)).astype(BF16)
    o += IDX_WIDTH
    kit_ref[0] = tproj(o, o + IDX_DIM)
    wit_ref[0] = tproj(o + IDX_DIM, o + IDX_DIM + IDX_HEADS) * (IDX_HEADS ** -0.5)

    ktok_ref[0, 0] = _dot(xn, wtok_ref[:, 0:w]).astype(BF16)
    kitok_ref[0, 0] = _dot(xn, wb_ref[:, 0:IDX_DIM]).astype(BF16)
    ga_ref[...] = _dot(xn, wtok_ref[:, w:2 * w])
    _hgrn_and_merge_gates(xn, wtok_ref, 2 * w, lbl_ref, layer, *tail)


def _inproj_sample(x2, g_pre, lb_logits, wa, wb, wc, layer, tm):
    m, d = x2.shape
    assert m % tm == 0
    row = lambda n, dt: (pl.BlockSpec((tm, n), lambda i: (i, 0)), jax.ShapeDtypeStruct((m, n), dt))
    full = lambda a: pl.BlockSpec(a.shape, lambda i: (0,) * a.ndim)
    outs = [row(ATT_WIDTH, BF16), row(ATT_WIDTH, F32), row(ATT_WIDTH, F32), row(ATT_WIDTH, F32),
            row(IDX_WIDTH, BF16), row(IDX_DIM, F32), row(IDX_HEADS, F32)] + [row(HG_WIDTH, F32)] * 5 + [row(d, F32)] * 2
    ins = [x2, g_pre, lb_logits, wa, wb, wc]
    return pl.pallas_call(
        functools.partial(_inproj_sample_kernel, layer=layer),
        grid=(m // tm,),
        in_specs=[pl.BlockSpec((tm, d), lambda i: (i, 0))] + [full(a) for a in ins[1:]],
        out_specs=[o[0] for o in outs],
        out_shape=[o[1] for o in outs],
        compiler_params=_cparams(("arbitrary",)),
        name="inproj_sample",
    )(*ins)


def _inproj_prompt(x2, g_pre, lb_logits, wt, wtok, wb, layer, tm, batch):
    m, d = x2.shape
    t = m // batch
    assert m % tm == 0 and t % tm == 0
    nt = t // tm
    row = lambda n, dt: (pl.BlockSpec((tm, n), lambda i: (i, 0)), jax.ShapeDtypeStruct((m, n), dt))
    full = lambda a: pl.BlockSpec(a.shape, lambda i: (0,) * a.ndim)
    fmaj = lambda n, dt: (pl.BlockSpec((1, n, tm), lambda i: (i // nt, 0, i % nt)),
                          jax.ShapeDtypeStruct((batch, n, t), dt))
    fchunk = lambda n: (pl.BlockSpec((1, 1, n, tm), lambda i: (i // nt, i % nt, 0, 0)),
                        jax.ShapeDtypeStruct((batch, nt, n, tm), BF16))
    tchunk = lambda n: (pl.BlockSpec((1, 1, tm, n), lambda i: (i // nt, i % nt, 0, 0)),
                        jax.ShapeDtypeStruct((batch, nt, tm, n), BF16))
    outs = [fmaj(ATT_WIDTH, BF16), fmaj(ATT_WIDTH, F32), tchunk(ATT_WIDTH), fmaj(ATT_WIDTH, F32), fchunk(ATT_WIDTH),
            row(ATT_WIDTH, F32), fmaj(IDX_WIDTH, BF16), fmaj(IDX_DIM, F32), tchunk(IDX_DIM), fmaj(IDX_HEADS, F32)]
    outs += [row(HG_WIDTH, F32)] * 5 + [row(d, F32)] * 2
    ins = [x2, g_pre, lb_logits, wt, wtok, wb]
    return pl.pallas_call(
        functools.partial(_inproj_prompt_kernel, layer=layer),
        grid=(m // tm,),
        in_specs=[pl.BlockSpec((tm, d), lambda i: (i, 0))] + [full(a) for a in ins[1:]],
        out_specs=[o[0] for o in outs],
        out_shape=[o[1] for o in outs],
        compiler_params=_cparams(("arbitrary",)),
        name="inproj_prompt",
    )(*ins)


def _f2key(x):
    b = lax.bitcast_convert_type(x, I32)
    return b ^ ((b >> 31) & INT_MAG)


def _key2f(k):
    return lax.bitcast_convert_type(k ^ ((k >> 31) & INT_MAG), F32)


def _select_topk(count, rewrite, top, n_valid, mx, mn, last_pos, index_steps):
    topf = float(top)

    small = n_valid <= top
    lo0 = _f2key(mn)
    hi0 = _f2key(mx) + 1
    conv0 = jnp.logical_and(jnp.logical_not(small), lo0 + 1 >= hi0)
    done0 = jnp.logical_or(small, conv0)
    thr0 = mn
    state0 = (lo0, hi0, jnp.zeros(mn.shape, F32), thr0, done0.astype(I32), conv0.astype(I32),
              jnp.int32(0), jnp.sum(1.0 - done0.astype(F32)))

    def cond(st):
        return st[-1] > 0.0

    def body(st):
        lo, hi, ghi, thr, done_i, tie_i, it, _ = st
        done = done_i > 0
        lof, hif = _key2f(lo), _key2f(hi)
        midf = 0.5 * lof + 0.5 * hif
        use_f = jnp.logical_and(it < FLOAT_BISECT_STEPS, jnp.logical_and(midf > lof, midf < hif))
        mid = jnp.where(use_f, _f2key(midf), (lo & hi) + ((lo ^ hi) >> 1))
        tmid = _key2f(mid)
        g = count(lambda x, kpos: x >= tmid)
        live = jnp.logical_not(done)
        is_eq = jnp.logical_and(live, g == topf)
        up = jnp.logical_and(live, g > topf)
        down = jnp.logical_and(live, g < topf)
        lo = jnp.where(up, mid, lo)
        hi = jnp.where(down, mid, hi)
        ghi = jnp.where(down, g, ghi)
        thr = jnp.where(is_eq, tmid, thr)
        done = jnp.logical_or(done, is_eq)
        conv = jnp.logical_and(jnp.logical_not(done), lo + 1 >= hi)
        thr = jnp.where(conv, _key2f(lo), thr)
        done = jnp.logical_or(done, conv)
        tie_i = jnp.where(conv, 1, tie_i)
        return (lo, hi, ghi, thr, done.astype(I32), tie_i, it + 1, jnp.sum(1.0 - done.astype(F32)))

    _, _, ghi, thr, _, tie_i, _, _ = lax.while_loop(cond, body, state0)

    tie = tie_i > 0
    n_tie = jnp.sum(tie_i.astype(F32))

    @pl.when(n_tie > 0.0)
    def _():
        need = topf - ghi
        jlo0 = jnp.full(mn.shape, -1, I32)
        jhi0 = jnp.full(mn.shape, 0, I32) + last_pos

        def jbody(_, carry):
            jlo, jhi = carry
            jm = (jlo + jhi) >> 1
            cnt = count(lambda x, kpos: jnp.logical_and(x == thr, kpos <= jm))
            ok = cnt >= need
            return jnp.where(ok, jlo, jm), jnp.where(ok, jm, jhi)

        _, jcut = lax.fori_loop(0, index_steps, jbody, (jlo0, jhi0))
        rewrite(lambda x, kpos: jnp.where(
            jnp.logical_and(jnp.logical_and(tie, x == thr), kpos > jcut), NEG_INF, x))

    rewrite(lambda x, kpos: jnp.where(x >= thr, 0.0, NEG_INF))


def _index_steps(n_positions):
    return max(1, (n_positions - 1).bit_length()) + 1


def _attn_prompt_kernel(qit_ref, wit_ref, kitok_ref, qt_ref, ktok_ref, vt_ref, o_ref,
                        sc_ref, qis_ref, qs_ref, m_ref, acc_ref, s_ref, d_ref, *, qb, kc, top):
    i = pl.program_id(1)
    nkc = ((i + 1) * qb + kc - 1) // kc

    for h in range(IDX_HEADS):
        qis_ref[:, h * qb:(h + 1) * qb] = qit_ref[0, h * IDX_DIM:(h + 1) * IDX_DIM, :]
    feat = lax.broadcasted_iota(I32, (LANES, qb), 0)
    for hp in range(NPAIR):
        qp = qt_ref[0, hp * LANES:(hp + 1) * LANES, :].astype(F32)
        qs_ref[hp] = jnp.concatenate([jnp.where(feat < HEAD_DIM, qp, 0.0),
                                      jnp.where(feat >= HEAD_DIM, qp, 0.0)], axis=1).astype(BF16)

    wit = wit_ref[0]
    wrows = [wit[h:h + 1, :] for h in range(IDX_HEADS)]
    qpos = i * qb + lax.broadcasted_iota(I32, (1, qb), 1)
    krow = lax.broadcasted_iota(I32, (kc, 1), 0)

    def kpos_of(c):
        return c * kc + krow

    half = IDX_HEADS * qb // 2

    def score_a(c, slot):
        kic = kitok_ref[0, c]
        d_ref[slot, :, 0:half] = _dot(kic, qis_ref[:, 0:half])
        d_ref[slot, :, half:2 * half] = _dot(kic, qis_ref[:, half:2 * half])
        return ()

    def score_b(c, slot, _, carry):
        mxp, mnp = carry
        acc = _tree(lambda a, b: a + b,
                    [wrows[h] * jnp.maximum(d_ref[slot, :, h * qb:(h + 1) * qb], 0.0) for h in range(IDX_HEADS)])
        valid = kpos_of(c) <= qpos
        x = jnp.where(valid, acc, NEG_INF)
        sc_ref[c] = x
        mxp = jnp.maximum(mxp, _fold_rows(x, jnp.maximum))
        mnp = jnp.minimum(mnp, _fold_rows(jnp.where(valid, acc, -NEG_INF), jnp.minimum))
        return mxp, mnp

    mxp, mnp = _two_stage(nkc, score_a, score_b, (jnp.full((SUBLANES, qb), NEG_INF, F32),
                                                  jnp.full((SUBLANES, qb), -NEG_INF, F32)))
    mx = jnp.max(mxp, axis=0, keepdims=True)
    mn = jnp.min(mnp, axis=0, keepdims=True)

    def count(pred):
        def body(c, acc):
            return acc + _fold_rows(jnp.where(pred(sc_ref[c], kpos_of(c)), 1.0, 0.0), lambda a, b: a + b)
        return jnp.sum(_chunk_loop2(nkc, body, jnp.zeros((SUBLANES, qb), F32)), axis=0, keepdims=True)

    def rewrite(fn):
        def body(c, _):
            sc_ref[c] = fn(sc_ref[c], kpos_of(c))
            return 0
        lax.fori_loop(0, nkc, body, 0)

    _select_topk(count, rewrite, top, qpos + 1, mx, mn, nkc * kc - 1, _index_steps(sc_ref.shape[0] * kc))

    m_ref[...] = jnp.full(m_ref.shape, M_INIT, F32)
    acc_ref[...] = jnp.zeros(acc_ref.shape, F32)
    ones = jnp.ones((BF16_ROWS, kc), BF16)

    def att_a(c, slot):
        bias = sc_ref[c]
        bias2 = jnp.concatenate([bias, bias], axis=1)
        cmax = []
        for hp in range(NPAIR):
            s = _dot(ktok_ref[0, c, :, hp * LANES:(hp + 1) * LANES], qs_ref[hp]) + bias2
            s_ref[slot, hp] = s
            cmax.append(jnp.max(_fold_rows(s, jnp.maximum), axis=0, keepdims=True))
        return cmax

    def att_b(c, slot, cmax, carry):
        for hp in range(NPAIR):
            m = m_ref[hp]
            m_new = jnp.maximum(m, cmax[hp])
            p = jnp.exp2(s_ref[slot, hp] - m_new).astype(BF16)
            v1 = jnp.concatenate([vt_ref[0, c, hp * LANES:(hp + 1) * LANES, :], ones], axis=0)
            acc_ref[hp] = jnp.exp2(m - m_new) * acc_ref[hp] + _dot(v1, p)
            m_ref[hp] = m_new
        return carry

    _two_stage(nkc, att_a, att_b, 0)

    for hp in range(NPAIR):
        acc = acc_ref[hp]
        out = acc[0:LANES] / acc[LANES:LANES + 1]
        out = jnp.where(feat < HEAD_DIM, out[:, 0:qb], out[:, qb:2 * qb])
        o_ref[0, :, hp * LANES:(hp + 1) * LANES] = out.T


def _attn_prompt(qit, wit, kitok, qt, ktok, vtc, qb):
    b, _, t = qt.shape
    nc, kc = ktok.shape[1], ktok.shape[2]
    top = min(TOPK_MAX, t // 4)
    assert t % qb == 0 and nc * kc == t and kc % qb == 0 and qb % LANES == 0
    qblk = lambda n: pl.BlockSpec((1, n, qb), lambda bi, i: (bi, 0, i))
    whole = lambda a: pl.BlockSpec((1,) + a.shape[1:], lambda bi, i: (bi, 0, 0, 0))
    return pl.pallas_call(
        functools.partial(_attn_prompt_kernel, qb=qb, kc=kc, top=top),
        grid=(b, t // qb),
        in_specs=[qblk(IDX_WIDTH), qblk(IDX_HEADS), whole(kitok), qblk(ATT_WIDTH), whole(ktok), whole(vtc)],
        out_specs=pl.BlockSpec((1, qb, ATT_WIDTH), lambda bi, i: (bi, i, 0)),
        out_shape=jax.ShapeDtypeStruct((b, t, ATT_WIDTH), F32),
        scratch_shapes=[pltpu.VMEM((nc, kc, qb), F32),
                        pltpu.VMEM((IDX_DIM, IDX_HEADS * qb), BF16),
                        pltpu.VMEM((NPAIR, LANES, 2 * qb), BF16),
                        pltpu.VMEM((NPAIR, 1, 2 * qb), F32),
                        pltpu.VMEM((NPAIR, LANES + BF16_ROWS, 2 * qb), F32),
                        pltpu.VMEM((2, NPAIR, kc, 2 * qb), F32),
                        pltpu.VMEM((2, kc, IDX_HEADS * qb), F32)],
        compiler_params=_cparams(("arbitrary", "arbitrary")),
        name="attn_prompt",
    )(qit, wit, kitok, qt, ktok, vtc)


def _split_pair(qp, lane):
    return jnp.concatenate([jnp.where(lane < HEAD_DIM, qp, 0.0), jnp.where(lane >= HEAD_DIM, qp, 0.0)], axis=0)


def _merge_pair(x, lane, rows):
    return jnp.where(lane < HEAD_DIM, x[0:rows], x[rows:2 * rows])


def _indexer_scores(d, wfull, rows):
    return _tree(lambda a, b: a + b,
                 [wfull[h] * jnp.maximum(d[h * rows:(h + 1) * rows], 0.0) for h in range(IDX_HEADS)])


def _index_sample_kernel(pt_ref, qi_ref, wi_ref, kin_ref, cki_hbm, bias_ref,
                         kibuf_ref, qis_ref, knew_ref, sem, *, nseq, tq, npages, top, unroll):
    b0 = pl.program_id(0) * nseq
    past = npages * PAGE
    nkc = npages + 1

    def page_copy(sq, p):
        return pltpu.make_async_copy(cki_hbm.at[pt_ref[b0 + sq, p]], kibuf_ref.at[sq, p], sem)

    for sq in range(nseq):
        lax.fori_loop(0, npages, lambda p, _, sq=sq: (page_copy(sq, p).start(), 0)[1], 0)

    knew_ref[...] = jnp.zeros(knew_ref.shape, F32)
    wfull = []
    for sq in range(nseq):
        knew_ref[sq, 0:tq, :] = kin_ref[sq]
        qi = qi_ref[sq].astype(F32)
        for h in range(IDX_HEADS):
            qis_ref[sq, h * tq:(h + 1) * tq, :] = qi[:, h * IDX_DIM:(h + 1) * IDX_DIM]
        wi = wi_ref[sq]
        wfull.append([jnp.broadcast_to(wi[:, h:h + 1], (tq, PAGE)) for h in range(IDX_HEADS)])
    qpos_seq = past + lax.broadcasted_iota(I32, (tq, 1), 0)
    qpos = jnp.concatenate([qpos_seq] * nseq, axis=0)
    kpos = (lax.broadcasted_iota(I32, (nkc, 1, PAGE), 0) * PAGE + lax.broadcasted_iota(I32, (nkc, 1, PAGE), 2))
    valid_new = kpos[npages] <= qpos_seq

    for sq in range(nseq):
        lax.fori_loop(0, npages, lambda p, _, sq=sq: (page_copy(sq, p).wait(), 0)[1], 0)

    qis = [qis_ref[sq].astype(BF16) for sq in range(nseq)]
    rows = [slice(sq * tq, (sq + 1) * tq) for sq in range(nseq)]

    def score(cc, carry):
        carry = list(carry)
        for sq in range(nseq):
            mxp, mnp = carry[sq]
            for j in range(unroll):
                c = cc * unroll + j
                acc = _indexer_scores(_dot(qis[sq], kibuf_ref[sq, c].astype(BF16)), wfull[sq], tq)
                bias_ref[0, c, rows[sq], :] = acc
                mxp = jnp.maximum(mxp, acc)
                mnp = jnp.minimum(mnp, acc)
            carry[sq] = (mxp, mnp)
        return tuple(carry)

    init = (jnp.full((tq, PAGE), NEG_INF, F32), jnp.full((tq, PAGE), -NEG_INF, F32))
    parts = lax.fori_loop(0, npages // unroll, score, (init,) * nseq)
    mxs, mns = [], []
    for sq in range(nseq):
        acc = _indexer_scores(_dot_nt(qis[sq], knew_ref[sq].astype(BF16)), wfull[sq], tq)
        bias_ref[0, npages, rows[sq], :] = jnp.where(valid_new, acc, NEG_INF)
        mxs.append(jnp.max(jnp.maximum(parts[sq][0], jnp.where(valid_new, acc, NEG_INF)), axis=-1, keepdims=True))
        mns.append(jnp.min(jnp.minimum(parts[sq][1], jnp.where(valid_new, acc, -NEG_INF)), axis=-1, keepdims=True))
    mx = jnp.concatenate(mxs, axis=0)
    mn = jnp.concatenate(mns, axis=0)

    def count(pred):
        ind = jnp.where(pred(bias_ref[0], kpos), 1.0, 0.0)
        return jnp.sum(jnp.sum(ind, axis=0), axis=-1, keepdims=True)

    def rewrite(fn):
        bias_ref[0] = fn(bias_ref[0], kpos)

    _select_topk(count, rewrite, top, qpos + 1, mx, mn, nkc * PAGE - 1, _index_steps(nkc * PAGE))


def _index_sample(page_table, qi, wi, ki_new, cki_t, nseq):
    b, tq, _ = qi.shape
    npages = page_table.shape[1]
    top = min(TOPK_MAX, (npages * PAGE + tq) // 4)
    nkc = npages + 1
    unroll = 8 if npages % 8 == 0 else 1
    assert tq <= PAGE and b % nseq == 0 and tq % SUBLANES == 0
    blk = lambda n: pl.BlockSpec((nseq, tq, n), lambda bi, pt: (bi, 0, 0))
    grid_spec = pltpu.PrefetchScalarGridSpec(
        num_scalar_prefetch=1,
        grid=(b // nseq,),
        in_specs=[blk(IDX_WIDTH), blk(IDX_HEADS), blk(IDX_DIM), pl.BlockSpec(memory_space=pl.ANY)],
        out_specs=pl.BlockSpec((1, nkc, nseq * tq, PAGE), lambda bi, pt: (bi, 0, 0, 0)),
        scratch_shapes=[pltpu.VMEM((nseq, npages, IDX_DIM, PAGE), F32),
                        pltpu.VMEM((nseq, IDX_HEADS * tq, IDX_DIM), F32),
                        pltpu.VMEM((nseq, PAGE, IDX_DIM), F32),
                        pltpu.SemaphoreType.DMA(())],
    )
    return pl.pallas_call(
        functools.partial(_index_sample_kernel, nseq=nseq, tq=tq, npages=npages, top=top, unroll=unroll),
        grid_spec=grid_spec,
        out_shape=jax.ShapeDtypeStruct((b // nseq, nkc, nseq * tq, PAGE), F32),
        compiler_params=_cparams(("arbitrary",)),
        name="index_sample",
    )(page_table, qi, wi, ki_new, cki_t)


def _attn_sample_kernel(pt_ref, q_ref, kn_ref, vn_ref, bias_ref, ck_hbm, cv_hbm, o_ref,
                        kbuf_ref, vbuf_ref, qs_ref, knew_ref, vnew_ref, m_ref, l_ref, acc_ref, s_ref, sems,
                        *, tq, gp, npages, nb):
    b = pl.program_id(0)
    g = pl.program_id(1)
    ng = npages // gp
    step = b * ng + g
    slot = step % 2

    def group_copies(bb, gg, sl):
        out = []
        for j in range(gp):
            page = pt_ref[bb, gg * gp + j]
            out.append(pltpu.make_async_copy(ck_hbm.at[page], kbuf_ref.at[sl, j], sems.at[0, sl]))
            out.append(pltpu.make_async_copy(cv_hbm.at[page], vbuf_ref.at[sl, j], sems.at[1, sl]))
        return out

    @pl.when(step == 0)
    def _():
        for cp in group_copies(b, g, slot):
            cp.start()

    @pl.when(step + 1 < nb * ng)
    def _():
        nxt = step + 1
        for cp in group_copies(nxt // ng, nxt % ng, 1 - slot):
            cp.start()

    lane = lax.broadcasted_iota(I32, (tq, LANES), 1)

    @pl.when(g == 0)
    def _():
        q = q_ref[0].astype(F32)
        for hp in range(NPAIR):
            qs_ref[hp] = _split_pair(q[:, hp * LANES:(hp + 1) * LANES], lane)
        m_ref[...] = jnp.full(m_ref.shape, M_INIT, F32)
        l_ref[...] = jnp.zeros(l_ref.shape, F32)
        acc_ref[...] = jnp.zeros(acc_ref.shape, F32)

    for cp in group_copies(b, g, slot):
        cp.wait()

    def bias2(c):
        bias = bias_ref[0, c]
        return jnp.concatenate([bias, bias], axis=0)

    def stage_scores(hp, s_list):
        for j, s in enumerate(s_list):
            s_ref[hp, j] = s
        return jnp.max(_tree(jnp.maximum, s_list), axis=-1, keepdims=True)

    def apply_update(hp, n, cmax, pv_of):
        m = m_ref[hp]
        m_new = jnp.maximum(m, cmax)
        alpha = jnp.exp(m - m_new)
        ps = [jnp.exp(s_ref[hp, j] - m_new) for j in range(n)]
        l_ref[hp] = alpha * l_ref[hp] + jnp.sum(_tree(lambda a, b: a + b, ps), axis=-1, keepdims=True)
        acc_ref[hp] = alpha * acc_ref[hp] + _tree(lambda a, b: a + b,
                                                  [pv_of(j, p.astype(BF16)) for j, p in enumerate(ps)])
        m_ref[hp] = m_new

    pairs = [slice(hp * LANES, (hp + 1) * LANES) for hp in range(NPAIR)]
    biases = [bias2(g * gp + j) for j in range(gp)]
    cmax = [stage_scores(hp, [_dot(qs_ref[hp].astype(BF16), kbuf_ref[slot, j, pairs[hp], :].astype(BF16)) + biases[j]
                              for j in range(gp)]) for hp in range(NPAIR)]
    for hp in range(NPAIR):
        apply_update(hp, gp, cmax[hp],
                     lambda j, p, hp=hp: _dot_nt(p, vbuf_ref[slot, j, pairs[hp], :].astype(BF16)))

    @pl.when(g == ng - 1)
    def _():
        knew_ref[...] = jnp.zeros(knew_ref.shape, F32)
        vnew_ref[...] = jnp.zeros(vnew_ref.shape, F32)
        knew_ref[0:tq, :] = kn_ref[0]
        vnew_ref[0:tq, :] = vn_ref[0]
        bnew = bias2(npages)
        for hp in range(NPAIR):
            s = _dot_nt(qs_ref[hp].astype(BF16), knew_ref[:, pairs[hp]].astype(BF16)) + bnew
            apply_update(hp, 1, stage_scores(hp, [s]),
                         lambda j, p, hp=hp: _dot(p, vnew_ref[:, pairs[hp]].astype(BF16)))
            o_ref[0, :, pairs[hp]] = _merge_pair(acc_ref[hp] / l_ref[hp], lane, tq)


def _attn_sample(page_table, q, k_new, v_new, bias, ck_t, cv_t, gp):
    b, tq, _ = q.shape
    npages = page_table.shape[1]
    nseq = bias.shape[2] // tq
    assert npages % gp == 0 and bias.shape[1] == npages + 1 and tq <= PAGE
    blk = lambda n: pl.BlockSpec((1, tq, n), lambda bi, gi, pt: (bi, 0, 0))
    grid_spec = pltpu.PrefetchScalarGridSpec(
        num_scalar_prefetch=1,
        grid=(b, npages // gp),
        in_specs=[blk(ATT_WIDTH), blk(ATT_WIDTH), blk(ATT_WIDTH),
                  pl.BlockSpec((1, npages + 1, tq, PAGE), lambda bi, gi, pt: (bi // nseq, 0, bi % nseq, 0)),
                  pl.BlockSpec(memory_space=pl.ANY), pl.BlockSpec(memory_space=pl.ANY)],
        out_specs=blk(ATT_WIDTH),
        scratch_shapes=[pltpu.VMEM((2, gp, ATT_WIDTH, PAGE), F32),
                        pltpu.VMEM((2, gp, ATT_WIDTH, PAGE), F32),
                        pltpu.VMEM((NPAIR, 2 * tq, LANES), F32),
                        pltpu.VMEM((PAGE, ATT_WIDTH), F32),
                        pltpu.VMEM((PAGE, ATT_WIDTH), F32),
                        pltpu.VMEM((NPAIR, 2 * tq, 1), F32),
                        pltpu.VMEM((NPAIR, 2 * tq, 1), F32),
                        pltpu.VMEM((NPAIR, 2 * tq, LANES), F32),
                        pltpu.VMEM((NPAIR, gp, 2 * tq, PAGE), F32),
                        pltpu.SemaphoreType.DMA((2, 2))],
    )
    return pl.pallas_call(
        functools.partial(_attn_sample_kernel, tq=tq, gp=gp, npages=npages, nb=b),
        grid_spec=grid_spec,
        out_shape=jax.ShapeDtypeStruct((b, tq, ATT_WIDTH), F32),
        compiler_params=_cparams(("arbitrary", "arbitrary")),
        name="attn_sample",
    )(page_table, q, k_new, v_new, bias, ck_t, cv_t)


def _cumsum_rows(x):
    n = x.shape[0]
    row = lax.broadcasted_iota(I32, x.shape, 0)
    sh = 1
    while sh < n:
        x = x + jnp.where(row >= sh, pltpu.roll(x, sh, 0), 0.0)
        sh *= 2
    return x


def _hgrn_kernel(hq_ref, lf_ref, hk_ref, hv_ref, hg_ref, s0_ref, ng_ref, o_ref, sout_ref,
                 st_ref, cum_ref, *, tb, c, sb):
    t = pl.program_id(2)
    nt = pl.num_programs(2)

    @pl.when(t == 0)
    def _():
        st_ref[...] = s0_ref[0, 0].T

    row8 = lax.broadcasted_iota(I32, (SUBLANES, 1), 0)
    row_c = lax.broadcasted_iota(I32, (c, 1), 0)
    nsb = c // sb

    for j in range(tb // c):
        rows = slice(j * c, (j + 1) * c)
        lf = lf_ref[0, rows, :]
        q = hq_ref[0, rows, :]
        k = hk_ref[0, rows, :]
        v = hv_ref[0, rows, :]
        cum = _cumsum_rows(lf) * LOG2E
        cum_ref[...] = cum
        last = cum[c - 1:c, :]
        st = st_ref[...]
        o = _dot_nt((q * jnp.exp2(cum)).astype(BF16), st.astype(BF16))

        vb = v.astype(BF16)
        o_parts = []
        a_rows = []
        for i in range(nsb):
            base = i * sb
            groups = [slice(base + r, base + r + SUBLANES) for r in range(0, sb, SUBLANES)]
            od = [jnp.zeros((SUBLANES, HG_DV), F32) for _ in groups]
            for s in range(sb):
                sr = base + s
                row = slice(j * c + sr, j * c + sr + 1)
                cum_s, k_s, v_s = cum_ref[sr:sr + 1, :], hk_ref[0, row, :], hv_ref[0, row, :]
                for gi, rr in enumerate(groups):
                    first = gi * SUBLANES
                    if first + SUBLANES <= s:
                        continue
                    ex = cum[rr] - cum_s
                    if first <= s:
                        ex = jnp.where(row8 >= s - first, ex, NEG_INF)
                    a_col = jnp.sum(jnp.exp2(ex) * (q[rr] * k_s), axis=-1, keepdims=True)
                    od[gi] = od[gi] + a_col * v_s
            o_parts.extend(od)
            if i > 0:
                c0 = cum[base:base + 1, :] - lf[base:base + 1, :] * LOG2E
                qs = (q[base:base + sb] * jnp.exp2(cum[base:base + sb] - c0)).astype(BF16)
                ks = (k * jnp.exp2(jnp.where(row_c < base, c0 - cum, NEG_INF))).astype(BF16)
                a_rows.append(_dot_nt(qs, ks))
            elif nsb > 1:
                a_rows.append(jnp.zeros((sb, c), F32))
        o = o + jnp.concatenate(o_parts, axis=0)
        if nsb > 1:
            o = o + _dot(jnp.concatenate(a_rows, axis=0).astype(BF16), vb)

        kd = (k * jnp.exp2(last - cum)).astype(BF16)
        st_ref[...] = st * jnp.exp2(last) + _dot_tn(vb, kd)

        ms = jnp.mean(o * o, axis=-1, keepdims=True)
        hg = hg_ref[0, rows, :]
        o_ref[0, rows, :] = (((o * lax.rsqrt(ms + RMS_EPS)) * ng_ref[...]) * (hg * _sigmoid(hg))).astype(BF16)

    @pl.when(t == nt - 1)
    def _():
        sout_ref[0, 0] = st_ref[...].T


def _hgrn(hq, lf, hk, hv, hg, s0, norm_g, tb, c):
    b, t, _ = hq.shape
    sb = 16
    assert t % tb == 0 and tb % c == 0 and c % sb == 0
    blk = pl.BlockSpec((1, tb, HG_DK), lambda bi, h, ti: (bi, ti, h))
    sblk = pl.BlockSpec((1, 1, HG_DK, HG_DV), lambda bi, h, ti: (bi, h, 0, 0))
    return pl.pallas_call(
        functools.partial(_hgrn_kernel, tb=tb, c=c, sb=sb),
        grid=(b, HG_HEADS, t // tb),
        in_specs=[blk, blk, blk, blk, blk, sblk, pl.BlockSpec((1, HG_DV), lambda bi, h, ti: (0, 0))],
        out_specs=[blk, sblk],
        out_shape=[jax.ShapeDtypeStruct((b, t, HG_WIDTH), BF16),
                   jax.ShapeDtypeStruct((b, HG_HEADS, HG_DK, HG_DV), F32)],
        scratch_shapes=[pltpu.VMEM((HG_DV, HG_DK), F32), pltpu.VMEM((c, HG_DK), F32)],
        compiler_params=_cparams(("arbitrary", "arbitrary", "arbitrary")),
        name="hgrn",
    )(hq, lf, hk, hv, hg, s0, norm_g)


def _merge_kernel(x_ref, att_ref, ga_ref, hb_ref, ma_ref, mb_ref, p_ref,
                  wa_ref, wb_ref, wo_ref, wpg_ref, wpp_ref, gp_ref, o_ref):
    ga = ga_ref[...]
    ya = _dot((att_ref[...] * (ga * _sigmoid(ga))).astype(BF16), wa_ref[...])
    yb = _dot(hb_ref[...], wb_ref[...])
    mixed = _sigmoid(ma_ref[...]) * ya + _sigmoid(mb_ref[...]) * yb
    u = _dot(mixed.astype(BF16), wo_ref[...])
    ms = jnp.mean(u * u, axis=-1, keepdims=True)
    h1 = x_ref[...] + (u * lax.rsqrt(ms + RMS_EPS)) * gp_ref[...]
    gate = _sigmoid(_dot(h1.astype(BF16), wpg_ref[...]))
    o_ref[...] = h1 + gate * _dot(p_ref[...].astype(BF16), wpp_ref[...])


def _merge(x2, att, ga, hb, ma, mb, p2, w_a, w_b, w_o, w_pg, w_pp, g_post, tm):
    m, d = x2.shape
    assert m % tm == 0
    row = lambda a: pl.BlockSpec((tm, a.shape[1]), lambda i: (i, 0))
    full = lambda a: pl.BlockSpec(a.shape, lambda i: (0, 0))
    acts = (x2, att, ga, hb, ma, mb, p2)
    wts = (w_a, w_b, w_o, w_pg, w_pp, g_post)
    return pl.pallas_call(
        _merge_kernel,
        grid=(m // tm,),
        in_specs=[row(a) for a in acts] + [full(w) for w in wts],
        out_specs=pl.BlockSpec((tm, d), lambda i: (i, 0)),
        out_shape=jax.ShapeDtypeStruct((m, d), F32),
        compiler_params=_cparams(("arbitrary",)),
        name="merge",
    )(*acts, *wts)


def _pick_rows(m, pref):
    return pref if m % pref == 0 else m


def kernel(x_prompt, x_sample, p_prompt, p_sample, cache_k, cache_v, cache_kidx, state_hgrn, page_table,
           g_pre, g_post, w_in, hgrn_lb_logits, hgrn_norm_g, w_branch_a, w_branch_b, w_out,
           w_ple_gate, w_ple_proj):
    depth = w_in.shape[0]
    bp, tp, d = x_prompt.shape
    bs, ts, _ = x_sample.shape
    nphys = cache_k.shape[1]
    hp, hs = x_prompt, x_sample
    outs = [[] for _ in range(8)]
    w = ATT_WIDTH
    a_end = 4 * w + IDX_WIDTH
    b_end = a_end + IDX_DIM + IDX_HEADS

    for i in range(depth):
        wi_ = w_in[i].astype(BF16)
        wa, wb, wc = wi_[:, :a_end], wi_[:, a_end:b_end], wi_[:, b_end:]
        wt = jnp.concatenate([wi_[:, 0:3 * w], wi_[:, 4 * w:b_end]], axis=1).T
        wtok = jnp.concatenate([wi_[:, w:2 * w], wi_[:, 3 * w:4 * w], wc], axis=1)
        g_pre_i = g_pre[i].reshape(1, d)
        wts = (w_branch_a[i].astype(BF16), w_branch_b[i].astype(BF16), w_out[i].astype(BF16),
               w_ple_gate[i].astype(BF16), w_ple_proj[i].astype(BF16), g_post[i].reshape(1, d))
        norm_g = hgrn_norm_g[i].reshape(1, HG_DV)

        mp = bp * tp
        kc = _pick_rows(tp, 256)
        (qt, kt, ktok, vt, vtc, ga, qit, kit, kitok, wit, hq, lf, hk, hv, hg, ma, mb) = _inproj_prompt(
            hp.reshape(mp, d), g_pre_i, hgrn_lb_logits, wt, wtok, wb, i, kc, bp)
        r3 = lambda a, b_=bp, t_=tp: a.reshape(b_, t_, a.shape[-1])
        att = _attn_prompt(qit, wit, kitok, qt, ktok, vtc, qb=128)
        s0 = jnp.zeros((bp, HG_HEADS, HG_DK, HG_DV), F32)
        hb, s_p = _hgrn(r3(hq), r3(lf), r3(hk), r3(hv), r3(hg), s0, norm_g, tb=_pick_rows(tp, 256), c=64)
        y = _merge(hp.reshape(mp, d), att.reshape(mp, w), ga, hb.reshape(mp, HG_WIDTH), ma, mb,
                   p_prompt[i].reshape(mp, -1), *wts, _pick_rows(mp, 256))
        hp = y.reshape(bp, tp, d)
        outs[0].append(kt.reshape(bp, N_HEADS, HEAD_DIM, tp).transpose(0, 3, 1, 2))
        outs[1].append(vt.reshape(bp, N_HEADS, HEAD_DIM, tp).transpose(0, 3, 1, 2))
        outs[2].append(kit.transpose(0, 2, 1))
        outs[3].append(s_p.astype(state_hgrn.dtype))

        ms = bs * ts
        (q, k, v, ga, qi, ki, wi, hq, lf, hk, hv, hg, ma, mb) = _inproj_sample(
            hs.reshape(ms, d), g_pre_i, hgrn_lb_logits, wa, wb, wc, i, _pick_rows(ms, 256))
        r3 = lambda a, b_=bs, t_=ts: a.reshape(b_, t_, a.shape[-1])
        cki_t = cache_kidx[i].transpose(0, 2, 1)
        ck_t = cache_k[i].transpose(0, 2, 3, 1).reshape(nphys, w, PAGE)
        cv_t = cache_v[i].transpose(0, 2, 3, 1).reshape(nphys, w, PAGE)
        bias = _index_sample(page_table, r3(qi), r3(wi), r3(ki), cki_t, nseq=4 if bs % 4 == 0 else 1)
        att = _attn_sample(page_table, r3(q), r3(k), r3(v), bias, ck_t, cv_t, gp=8)
        tpad = -ts % 16
        pad = lambda a: jnp.pad(r3(a), ((0, 0), (0, tpad), (0, 0)))
        hb, s_s = _hgrn(pad(hq), pad(lf), pad(hk), pad(hv), pad(hg), state_hgrn[i], norm_g,
                        tb=ts + tpad, c=ts + tpad)
        y = _merge(hs.reshape(ms, d), att.reshape(ms, w), ga, hb[:, :ts].reshape(ms, HG_WIDTH), ma, mb,
                   p_sample[i].reshape(ms, -1), *wts, _pick_rows(ms, 256))
        hs = y.reshape(bs, ts, d)
        outs[4].append(k.reshape(bs, ts, N_HEADS, HEAD_DIM))
        outs[5].append(v.reshape(bs, ts, N_HEADS, HEAD_DIM))
        outs[6].append(ki.reshape(bs, ts, IDX_DIM))
        outs[7].append(s_s.astype(state_hgrn.dtype))

    st = [jnp.stack(o) for o in outs]
    return (hp, hs, st[0], st[1], st[2], st[3], st[4], st[5], st[6], st[7])
```

```python
import functools

import jax
import jax.numpy as jnp
from jax import lax
from jax.experimental import pallas as pl
from jax.experimental.pallas import tpu as pltpu

F32 = jnp.float32
BF16 = jnp.bfloat16
I32 = jnp.int32

N_HEADS = 8
HEAD_DIM = 64
ATT_WIDTH = N_HEADS * HEAD_DIM
IDX_HEADS = 8
IDX_DIM = 64
IDX_WIDTH = IDX_HEADS * IDX_DIM
TOPK_MAX = 256
HG_HEADS = 4
HG_DK = 128
HG_DV = 128
HG_WIDTH = HG_HEADS * HG_DV
PAGE = 128
RMS_EPS = 1e-6

LANES = 128
SUBLANES = 8
BF16_ROWS = 16
VMEM_LIMIT = 56 * 1024 * 1024
NEG_INF = float("-inf")
M_INIT = -1e30
INT_MAG = 0x7FFFFFFF
NPAIR = N_HEADS // 2
LOG2E = 1.4426950408889634
FLOAT_BISECT_STEPS = 12
UNTESTED_PASSES = 8
KV_SLOTS = 3


def _cparams(sem):
    return pltpu.CompilerParams(dimension_semantics=sem, vmem_limit_bytes=VMEM_LIMIT)


def _sigmoid(x):
    return 1.0 / (1.0 + jnp.exp(-x))


def _dot(a, b):
    return jnp.dot(a, b, preferred_element_type=F32)


def _dot_nt(a, b):
    return lax.dot_general(a, b, (((1,), (1,)), ((), ())), preferred_element_type=F32)


def _dot_tn(a, b):
    return lax.dot_general(a, b, (((0,), (0,)), ((), ())), preferred_element_type=F32)


def _tree(op, xs):
    xs = list(xs)
    while len(xs) > 1:
        xs = [op(xs[i], xs[i + 1]) if i + 1 < len(xs) else xs[i] for i in range(0, len(xs), 2)]
    return xs[0]


def _fold_rows(x, op):
    return _tree(op, [x[r:r + SUBLANES] for r in range(0, x.shape[0], SUBLANES)])


def _two_stage(n, stage_a, stage_b, carry):
    def two(j, st):
        a0, carry = st
        a1 = stage_a(2 * j + 1, 1)
        carry = stage_b(2 * j, 0, a0, carry)
        a0 = stage_a(2 * j + 2, 0)
        carry = stage_b(2 * j + 1, 1, a1, carry)
        return a0, carry

    npairs = (n - 1) // 2
    a0, carry = lax.fori_loop(0, npairs, two, (stage_a(0, 0), carry))
    last = 2 * npairs

    def tail2():
        a1 = stage_a(last + 1, 1)
        return stage_b(last + 1, 1, a1, stage_b(last, 0, a0, carry))

    return lax.cond(last + 1 < n, tail2, lambda: stage_b(last, 0, a0, carry))


def _chunk_loop2(n, body, carry):
    carry = lax.fori_loop(0, n // 2, lambda j, cr: body(2 * j + 1, body(2 * j, cr)), carry)
    return lax.cond(n % 2 == 1, lambda: body(n - 1, carry), lambda: carry)


def _normed(x_ref, g_ref):
    x = x_ref[...]
    var = jnp.mean(x * x, axis=-1, keepdims=True)
    return ((x * lax.rsqrt(var + RMS_EPS)) * g_ref[...]).astype(BF16)


def _hgrn_and_merge_gates(xn, w_ref, off, lbl_ref, layer, hq_ref, lf_ref, hk_ref, hv_ref, hg_ref, ma_ref, mb_ref):
    def proj(lo, hi):
        return _dot(xn, w_ref[:, off + lo:off + hi])

    lg = lbl_ref[...]
    e = jnp.exp(lg - jnp.max(lg, axis=0, keepdims=True))
    sm = e / jnp.sum(e, axis=0, keepdims=True)
    lb = jnp.sum(sm[:layer + 1], axis=0, keepdims=True)
    one_m = 1.0 - lb

    hw = HG_WIDTH
    hq = proj(0, hw)
    hq_ref[...] = hq * _sigmoid(hq)
    zf = proj(hw, 2 * hw)
    lf_ref[...] = jnp.log(lb + one_m * _sigmoid(zf))
    hk_ref[...] = one_m * _sigmoid(-zf)
    hv_ref[...] = proj(2 * hw, 3 * hw)
    hg_ref[...] = proj(3 * hw, 4 * hw)
    d = ma_ref.shape[-1]
    ma_ref[...] = proj(4 * hw, 4 * hw + d)
    mb_ref[...] = proj(4 * hw + d, 4 * hw + 2 * d)


def _inproj_sample_kernel(x_ref, g_ref, lbl_ref, wa_ref, wb_ref, wc_ref,
                          q_ref, k_ref, v_ref, ga_ref, qi_ref, ki_ref, wi_ref, *tail, layer):
    xn = _normed(x_ref, g_ref)
    w = ATT_WIDTH
    q_ref[...] = (_dot(xn, wa_ref[:, 0:w]) * (HEAD_DIM ** -0.5)).astype(BF16)
    k_ref[...] = _dot(xn, wa_ref[:, w:2 * w])
    v_ref[...] = _dot(xn, wa_ref[:, 2 * w:3 * w])
    ga_ref[...] = _dot(xn, wa_ref[:, 3 * w:4 * w])
    qi_ref[...] = (_dot(xn, wa_ref[:, 4 * w:4 * w + IDX_WIDTH]) * (IDX_DIM ** -0.5)).astype(BF16)
    kiwi = _dot(xn, wb_ref[...])
    ki_ref[...] = kiwi[:, :IDX_DIM]
    wi_ref[...] = kiwi[:, IDX_DIM:IDX_DIM + IDX_HEADS] * (IDX_HEADS ** -0.5)
    _hgrn_and_merge_gates(xn, wc_ref, 0, lbl_ref, layer, *tail)


def _inproj_prompt_kernel(x_ref, g_ref, lbl_ref, wt_ref, wtok_ref, wb_ref,
                          qt_ref, kt_ref, ktok_ref, vt_ref, vtc_ref, ga_ref, qit_ref, kit_ref, kitok_ref, wit_ref,
                          *tail, layer):
    xn = _normed(x_ref, g_ref)
    w = ATT_WIDTH

    def tproj(lo, hi):
        return _dot_nt(wt_ref[lo:hi, :], xn)

    qt_ref[0] = (tproj(0, w) * (HEAD_DIM ** -0.5 * LOG2E)).astype(BF16)
    kt_ref[0] = tproj(w, 2 * w)
    vt = tproj(2 * w, 3 * w)
    vt_ref[0] = vt
    vtc_ref[0, 0] = vt.astype(BF16)
    o = 3 * w
    qit_ref[0] = (tproj(o, o + IDX_WIDTH) * (IDX_DIM ** -0.5)).astype(BF16)
    o += IDX_WIDTH
    kit_ref[0] = tproj(o, o + IDX_DIM)
    wit_ref[0] = tproj(o + IDX_DIM, o + IDX_DIM + IDX_HEADS) * (IDX_HEADS ** -0.5)

    ktok_ref[0, 0] = _dot(xn, wtok_ref[:, 0:w]).astype(BF16)
    kitok_ref[0, 0] = _dot(xn, wb_ref[:, 0:IDX_DIM]).astype(BF16)
    ga_ref[...] = _dot(xn, wtok_ref[:, w:2 * w])
    _hgrn_and_merge_gates(xn, wtok_ref, 2 * w, lbl_ref, layer, *tail)


def _inproj_sample(x2, g_pre, lb_logits, wa, wb, wc, layer, tm):
    m, d = x2.shape
    assert m % tm == 0
    row = lambda n, dt: (pl.BlockSpec((tm, n), lambda i: (i, 0)), jax.ShapeDtypeStruct((m, n), dt))
    full = lambda a: pl.BlockSpec(a.shape, lambda i: (0,) * a.ndim)
    outs = [row(ATT_WIDTH, BF16), row(ATT_WIDTH, F32), row(ATT_WIDTH, F32), row(ATT_WIDTH, F32),
            row(IDX_WIDTH, BF16), row(IDX_DIM, F32), row(IDX_HEADS, F32)] + [row(HG_WIDTH, F32)] * 5 + [row(d, F32)] * 2
    ins = [x2, g_pre, lb_logits, wa, wb, wc]
    return pl.pallas_call(
        functools.partial(_inproj_sample_kernel, layer=layer),
        grid=(m // tm,),
        in_specs=[pl.BlockSpec((tm, d), lambda i: (i, 0))] + [full(a) for a in ins[1:]],
        out_specs=[o[0] for o in outs],
        out_shape=[o[1] for o in outs],
        compiler_params=_cparams(("arbitrary",)),
        name="inproj_sample",
    )(*ins)


def _inproj_prompt(x2, g_pre, lb_logits, wt, wtok, wb, layer, tm, batch):
    m, d = x2.shape
    t = m // batch
    assert m % tm == 0 and t % tm == 0
    nt = t // tm
    row = lambda n, dt: (pl.BlockSpec((tm, n), lambda i: (i, 0)), jax.ShapeDtypeStruct((m, n), dt))
    full = lambda a: pl.BlockSpec(a.shape, lambda i: (0,) * a.ndim)
    fmaj = lambda n, dt: (pl.BlockSpec((1, n, tm), lambda i: (i // nt, 0, i % nt)),
                          jax.ShapeDtypeStruct((batch, n, t), dt))
    fchunk = lambda n: (pl.BlockSpec((1, 1, n, tm), lambda i: (i // nt, i % nt, 0, 0)),
                        jax.ShapeDtypeStruct((batch, nt, n, tm), BF16))
    tchunk = lambda n: (pl.BlockSpec((1, 1, tm, n), lambda i: (i // nt, i % nt, 0, 0)),
                        jax.ShapeDtypeStruct((batch, nt, tm, n), BF16))
    outs = [fmaj(ATT_WIDTH, BF16), fmaj(ATT_WIDTH, F32), tchunk(ATT_WIDTH), fmaj(ATT_WIDTH, F32), fchunk(ATT_WIDTH),
            row(ATT_WIDTH, F32), fmaj(IDX_WIDTH, BF16), fmaj(IDX_DIM, F32), tchunk(IDX_DIM), fmaj(IDX_HEADS, F32)]
    outs += [row(HG_WIDTH, F32)] * 5 + [row(d, F32)] * 2
    ins = [x2, g_pre, lb_logits, wt, wtok, wb]
    return pl.pallas_call(
        functools.partial(_inproj_prompt_kernel, layer=layer),
        grid=(m // tm,),
        in_specs=[pl.BlockSpec((tm, d), lambda i: (i, 0))] + [full(a) for a in ins[1:]],
        out_specs=[o[0] for o in outs],
        out_shape=[o[1] for o in outs],
        compiler_params=_cparams(("arbitrary",)),
        name="inproj_prompt",
    )(*ins)


def _f2key(x):
    b = lax.bitcast_convert_type(x, I32)
    return b ^ ((b >> 31) & INT_MAG)


def _key2f(k):
    return lax.bitcast_convert_type(k ^ ((k >> 31) & INT_MAG), F32)


def _select_topk(count, rewrite, top, n_valid, mx, mn, last_pos, index_steps):
    topf = float(top)

    small = n_valid <= top
    lo0 = _f2key(mn)
    hi0 = _f2key(mx) + 1
    conv0 = jnp.logical_and(jnp.logical_not(small), lo0 + 1 >= hi0)
    done0 = jnp.logical_or(small, conv0)
    thr0 = mn
    n_live0 = jnp.sum(1.0 - done0.astype(F32))
    state0 = (lo0, hi0, jnp.zeros(mn.shape, F32), thr0, done0.astype(I32), conv0.astype(I32),
              jnp.int32(0), n_live0, n_live0)

    def cond(st):
        return st[-2] > 0.0

    def body(st):
        lo, hi, ghi, thr, done_i, tie_i, it, _, n_live = st
        done = done_i > 0
        lof, hif = _key2f(lo), _key2f(hi)
        midf = 0.5 * lof + 0.5 * hif
        use_f = jnp.logical_and(it < FLOAT_BISECT_STEPS, jnp.logical_and(midf > lof, midf < hif))
        mid = jnp.where(use_f, _f2key(midf), (lo & hi) + ((lo ^ hi) >> 1))
        tmid = _key2f(mid)
        g = count(lambda x, kpos: x >= tmid)
        live = jnp.logical_not(done)
        is_eq = jnp.logical_and(live, g == topf)
        up = jnp.logical_and(live, g > topf)
        down = jnp.logical_and(live, g < topf)
        lo = jnp.where(up, mid, lo)
        hi = jnp.where(down, mid, hi)
        ghi = jnp.where(down, g, ghi)
        thr = jnp.where(is_eq, tmid, thr)
        done = jnp.logical_or(done, is_eq)
        conv = jnp.logical_and(jnp.logical_not(done), lo + 1 >= hi)
        thr = jnp.where(conv, _key2f(lo), thr)
        done = jnp.logical_or(done, conv)
        tie_i = jnp.where(conv, 1, tie_i)
        return (lo, hi, ghi, thr, done.astype(I32), tie_i, it + 1, n_live, jnp.sum(1.0 - done.astype(F32)))

    state = lax.cond(state0[-1] > 0.0,
                     lambda: lax.fori_loop(0, UNTESTED_PASSES, lambda _, st: body(st), state0),
                     lambda: state0)
    _, _, ghi, thr, _, tie_i, _, _, _ = lax.while_loop(cond, body, state)

    tie = tie_i > 0
    n_tie = jnp.sum(tie_i.astype(F32))

    @pl.when(n_tie > 0.0)
    def _():
        need = topf - ghi
        jlo0 = jnp.full(mn.shape, -1, I32)
        jhi0 = jnp.full(mn.shape, 0, I32) + last_pos

        def jbody(_, carry):
            jlo, jhi = carry
            jm = (jlo + jhi) >> 1
            cnt = count(lambda x, kpos: jnp.logical_and(x == thr, kpos <= jm))
            ok = cnt >= need
            return jnp.where(ok, jlo, jm), jnp.where(ok, jm, jhi)

        _, jcut = lax.fori_loop(0, index_steps, jbody, (jlo0, jhi0))
        rewrite(lambda x, kpos: jnp.where(
            jnp.logical_and(jnp.logical_and(tie, x == thr), kpos > jcut), NEG_INF, x))

    rewrite(lambda x, kpos: jnp.where(x >= thr, 0.0, NEG_INF))


def _index_steps(n_positions):
    return max(1, (n_positions - 1).bit_length()) + 1


def _attn_prompt_kernel(qit_ref, wit_ref, kitok_ref, qt_ref, ktok_ref, vt_ref, o_ref,
                        sc_ref, qis_ref, qs_ref, m_ref, acc_ref, s_ref, d_ref, *, qb, kc, top):
    i = pl.program_id(1)
    nkc = ((i + 1) * qb + kc - 1) // kc

    for h in range(IDX_HEADS):
        qis_ref[:, h * qb:(h + 1) * qb] = qit_ref[0, h * IDX_DIM:(h + 1) * IDX_DIM, :]
    feat = lax.broadcasted_iota(I32, (LANES, qb), 0)
    for hp in range(NPAIR):
        qp = qt_ref[0, hp * LANES:(hp + 1) * LANES, :].astype(F32)
        qs_ref[hp] = jnp.concatenate([jnp.where(feat < HEAD_DIM, qp, 0.0),
                                      jnp.where(feat >= HEAD_DIM, qp, 0.0)], axis=1).astype(BF16)

    wit = wit_ref[0]
    wrows = [wit[h:h + 1, :] for h in range(IDX_HEADS)]
    qpos = i * qb + lax.broadcasted_iota(I32, (1, qb), 1)
    krow = lax.broadcasted_iota(I32, (kc, 1), 0)

    def kpos_of(c):
        return c * kc + krow

    half = IDX_HEADS * qb // 2

    def score_a(c, slot):
        kic = kitok_ref[0, c]
        d_ref[slot, :, 0:half] = _dot(kic, qis_ref[:, 0:half])
        d_ref[slot, :, half:2 * half] = _dot(kic, qis_ref[:, half:2 * half])
        return ()

    def score_b(c, slot, _, carry):
        mxp, mnp = carry
        acc = _tree(lambda a, b: a + b,
                    [wrows[h] * jnp.maximum(d_ref[slot, :, h * qb:(h + 1) * qb], 0.0) for h in range(IDX_HEADS)])
        valid = kpos_of(c) <= qpos
        x = jnp.where(valid, acc, NEG_INF)
        sc_ref[c] = x
        mxp = jnp.maximum(mxp, _fold_rows(x, jnp.maximum))
        mnp = jnp.minimum(mnp, _fold_rows(jnp.where(valid, acc, -NEG_INF), jnp.minimum))
        return mxp, mnp

    mxp, mnp = _two_stage(nkc, score_a, score_b, (jnp.full((SUBLANES, qb), NEG_INF, F32),
                                                  jnp.full((SUBLANES, qb), -NEG_INF, F32)))
    mx = jnp.max(mxp, axis=0, keepdims=True)
    mn = jnp.min(mnp, axis=0, keepdims=True)

    def count(pred):
        def body(c, acc):
            return acc + _fold_rows(jnp.where(pred(sc_ref[c], kpos_of(c)), 1.0, 0.0), lambda a, b: a + b)
        return jnp.sum(_chunk_loop2(nkc, body, jnp.zeros((SUBLANES, qb), F32)), axis=0, keepdims=True)

    def rewrite(fn):
        def body(c, _):
            sc_ref[c] = fn(sc_ref[c], kpos_of(c))
            return 0
        lax.fori_loop(0, nkc, body, 0)

    _select_topk(count, rewrite, top, qpos + 1, mx, mn, nkc * kc - 1, _index_steps(sc_ref.shape[0] * kc))

    m_ref[...] = jnp.full(m_ref.shape, M_INIT, F32)
    acc_ref[...] = jnp.zeros(acc_ref.shape, F32)
    ones = jnp.ones((BF16_ROWS, kc), BF16)

    def att_a(c, slot):
        bias = sc_ref[c]
        bias2 = jnp.concatenate([bias, bias], axis=1)
        cmax = []
        for hp in range(NPAIR):
            s = _dot(ktok_ref[0, c, :, hp * LANES:(hp + 1) * LANES], qs_ref[hp]) + bias2
            s_ref[slot, hp] = s
            cmax.append(jnp.max(_fold_rows(s, jnp.maximum), axis=0, keepdims=True))
        return cmax

    def att_b(c, slot, cmax, carry):
        for hp in range(NPAIR):
            m = m_ref[hp]
            m_new = jnp.maximum(m, cmax[hp])
            p = jnp.exp2(s_ref[slot, hp] - m_new).astype(BF16)
            v1 = jnp.concatenate([vt_ref[0, c, hp * LANES:(hp + 1) * LANES, :], ones], axis=0)
            acc_ref[hp] = jnp.exp2(m - m_new) * acc_ref[hp] + _dot(v1, p)
            m_ref[hp] = m_new
        return carry

    _two_stage(nkc, att_a, att_b, 0)

    for hp in range(NPAIR):
        acc = acc_ref[hp]
        out = acc[0:LANES] / acc[LANES:LANES + 1]
        out = jnp.where(feat < HEAD_DIM, out[:, 0:qb], out[:, qb:2 * qb])
        o_ref[0, :, hp * LANES:(hp + 1) * LANES] = out.T


def _attn_prompt(qit, wit, kitok, qt, ktok, vtc, qb):
    b, _, t = qt.shape
    nc, kc = ktok.shape[1], ktok.shape[2]
    top = min(TOPK_MAX, t // 4)
    assert t % qb == 0 and nc * kc == t and kc % qb == 0 and qb % LANES == 0
    qblk = lambda n: pl.BlockSpec((1, n, qb), lambda bi, i: (bi, 0, i))
    whole = lambda a: pl.BlockSpec((1,) + a.shape[1:], lambda bi, i: (bi, 0, 0, 0), pipeline_mode=pl.Buffered(1))
    return pl.pallas_call(
        functools.partial(_attn_prompt_kernel, qb=qb, kc=kc, top=top),
        grid=(b, t // qb),
        in_specs=[qblk(IDX_WIDTH), qblk(IDX_HEADS), whole(kitok), qblk(ATT_WIDTH), whole(ktok), whole(vtc)],
        out_specs=pl.BlockSpec((1, qb, ATT_WIDTH), lambda bi, i: (bi, i, 0)),
        out_shape=jax.ShapeDtypeStruct((b, t, ATT_WIDTH), F32),
        scratch_shapes=[pltpu.VMEM((nc, kc, qb), F32),
                        pltpu.VMEM((IDX_DIM, IDX_HEADS * qb), BF16),
                        pltpu.VMEM((NPAIR, LANES, 2 * qb), BF16),
                        pltpu.VMEM((NPAIR, 1, 2 * qb), F32),
                        pltpu.VMEM((NPAIR, LANES + BF16_ROWS, 2 * qb), F32),
                        pltpu.VMEM((2, NPAIR, kc, 2 * qb), F32),
                        pltpu.VMEM((2, kc, IDX_HEADS * qb), F32)],
        compiler_params=_cparams(("arbitrary", "arbitrary")),
        name="attn_prompt",
    )(qit, wit, kitok, qt, ktok, vtc)


def _split_pair(qp, lane):
    return jnp.concatenate([jnp.where(lane < HEAD_DIM, qp, 0.0), jnp.where(lane >= HEAD_DIM, qp, 0.0)], axis=0)


def _merge_pair(x, lane, rows):
    return jnp.where(lane < HEAD_DIM, x[0:rows], x[rows:2 * rows])


def _indexer_scores(d, wfull, rows):
    return _tree(lambda a, b: a + b,
                 [wfull[h] * jnp.maximum(d[h * rows:(h + 1) * rows], 0.0) for h in range(IDX_HEADS)])


def _index_sample_kernel(pt_ref, qi_ref, wi_ref, kin_ref, cki_hbm, bias_ref,
                         kibuf_ref, qis_ref, knew_ref, sem, *, nseq, tq, npages, top, unroll):
    b0 = pl.program_id(0) * nseq
    past = npages * PAGE
    nkc = npages + 1

    def page_copy(sq, p):
        return pltpu.make_async_copy(cki_hbm.at[pt_ref[b0 + sq, p]], kibuf_ref.at[sq, p], sem)

    for sq in range(nseq):
        lax.fori_loop(0, npages, lambda p, _, sq=sq: (page_copy(sq, p).start(), 0)[1], 0)

    knew_ref[...] = jnp.zeros(knew_ref.shape, F32)
    wfull = []
    for sq in range(nseq):
        knew_ref[sq, 0:tq, :] = kin_ref[sq]
        qi = qi_ref[sq].astype(F32)
        for h in range(IDX_HEADS):
            qis_ref[sq, h * tq:(h + 1) * tq, :] = qi[:, h * IDX_DIM:(h + 1) * IDX_DIM]
        wi = wi_ref[sq]
        wfull.append([jnp.broadcast_to(wi[:, h:h + 1], (tq, PAGE)) for h in range(IDX_HEADS)])
    qpos_seq = past + lax.broadcasted_iota(I32, (tq, 1), 0)
    qpos = jnp.concatenate([qpos_seq] * nseq, axis=0)
    kpos = (lax.broadcasted_iota(I32, (nkc, 1, PAGE), 0) * PAGE + lax.broadcasted_iota(I32, (nkc, 1, PAGE), 2))
    valid_new = kpos[npages] <= qpos_seq

    for sq in range(nseq):
        lax.fori_loop(0, npages, lambda p, _, sq=sq: (page_copy(sq, p).wait(), 0)[1], 0)

    qis = [qis_ref[sq].astype(BF16) for sq in range(nseq)]
    rows = [slice(sq * tq, (sq + 1) * tq) for sq in range(nseq)]

    def score(cc, carry):
        carry = list(carry)
        for sq in range(nseq):
            mxp, mnp = carry[sq]
            for j in range(unroll):
                c = cc * unroll + j
                acc = _indexer_scores(_dot(qis[sq], kibuf_ref[sq, c].astype(BF16)), wfull[sq], tq)
                bias_ref[0, c, rows[sq], :] = acc
                mxp = jnp.maximum(mxp, acc)
                mnp = jnp.minimum(mnp, acc)
            carry[sq] = (mxp, mnp)
        return tuple(carry)

    init = (jnp.full((tq, PAGE), NEG_INF, F32), jnp.full((tq, PAGE), -NEG_INF, F32))
    parts = lax.fori_loop(0, npages // unroll, score, (init,) * nseq)
    mxs, mns = [], []
    for sq in range(nseq):
        acc = _indexer_scores(_dot_nt(qis[sq], knew_ref[sq].astype(BF16)), wfull[sq], tq)
        bias_ref[0, npages, rows[sq], :] = jnp.where(valid_new, acc, NEG_INF)
        mxs.append(jnp.max(jnp.maximum(parts[sq][0], jnp.where(valid_new, acc, NEG_INF)), axis=-1, keepdims=True))
        mns.append(jnp.min(jnp.minimum(parts[sq][1], jnp.where(valid_new, acc, -NEG_INF)), axis=-1, keepdims=True))
    mx = jnp.concatenate(mxs, axis=0)
    mn = jnp.concatenate(mns, axis=0)

    def count(pred):
        ind = jnp.where(pred(bias_ref[0], kpos), 1.0, 0.0)
        return jnp.sum(jnp.sum(ind, axis=0), axis=-1, keepdims=True)

    def rewrite(fn):
        bias_ref[0] = fn(bias_ref[0], kpos)

    _select_topk(count, rewrite, top, qpos + 1, mx, mn, nkc * PAGE - 1, _index_steps(nkc * PAGE))


def _index_sample(page_table, qi, wi, ki_new, cki_t, nseq):
    b, tq, _ = qi.shape
    npages = page_table.shape[1]
    top = min(TOPK_MAX, (npages * PAGE + tq) // 4)
    nkc = npages + 1
    unroll = 8 if npages % 8 == 0 else 1
    assert tq <= PAGE and b % nseq == 0 and tq % SUBLANES == 0
    blk = lambda n: pl.BlockSpec((nseq, tq, n), lambda bi, pt: (bi, 0, 0))
    grid_spec = pltpu.PrefetchScalarGridSpec(
        num_scalar_prefetch=1,
        grid=(b // nseq,),
        in_specs=[blk(IDX_WIDTH), blk(IDX_HEADS), blk(IDX_DIM), pl.BlockSpec(memory_space=pl.ANY)],
        out_specs=pl.BlockSpec((1, nkc, nseq * tq, PAGE), lambda bi, pt: (bi, 0, 0, 0)),
        scratch_shapes=[pltpu.VMEM((nseq, npages, IDX_DIM, PAGE), F32),
                        pltpu.VMEM((nseq, IDX_HEADS * tq, IDX_DIM), F32),
                        pltpu.VMEM((nseq, PAGE, IDX_DIM), F32),
                        pltpu.SemaphoreType.DMA(())],
    )
    return pl.pallas_call(
        functools.partial(_index_sample_kernel, nseq=nseq, tq=tq, npages=npages, top=top, unroll=unroll),
        grid_spec=grid_spec,
        out_shape=jax.ShapeDtypeStruct((b // nseq, nkc, nseq * tq, PAGE), F32),
        compiler_params=_cparams(("arbitrary",)),
        name="index_sample",
    )(page_table, qi, wi, ki_new, cki_t)


def _attn_sample_kernel(pt_ref, q_ref, kn_ref, vn_ref, bias_ref, ck_hbm, cv_hbm, o_ref,
                        kbuf_ref, vbuf_ref, qs_ref, knew_ref, vnew_ref, m_ref, l_ref, acc_ref, s_ref, sems,
                        *, tq, gp, npages, nb):
    b = pl.program_id(0)
    g = pl.program_id(1)
    ng = npages // gp
    step = b * ng + g
    slot = step % KV_SLOTS

    def group_copies(bb, gg, sl):
        out = []
        for j in range(gp):
            page = pt_ref[bb, gg * gp + j]
            out.append(pltpu.make_async_copy(ck_hbm.at[page], kbuf_ref.at[sl, j], sems.at[0, sl]))
            out.append(pltpu.make_async_copy(cv_hbm.at[page], vbuf_ref.at[sl, j], sems.at[1, sl]))
        return out

    def start_step(st):
        for cp in group_copies(st // ng, st % ng, st % KV_SLOTS):
            cp.start()

    ahead = KV_SLOTS - 1
    assert nb * ng >= ahead

    @pl.when(step == 0)
    def _():
        for st in range(ahead):
            start_step(st)

    @pl.when(step + ahead < nb * ng)
    def _():
        start_step(step + ahead)

    lane = lax.broadcasted_iota(I32, (tq, LANES), 1)

    @pl.when(g == 0)
    def _():
        q = q_ref[0].astype(F32)
        for hp in range(NPAIR):
            qs_ref[hp] = _split_pair(q[:, hp * LANES:(hp + 1) * LANES], lane)
        m_ref[...] = jnp.full(m_ref.shape, M_INIT, F32)
        l_ref[...] = jnp.zeros(l_ref.shape, F32)
        acc_ref[...] = jnp.zeros(acc_ref.shape, F32)

    for cp in group_copies(b, g, slot):
        cp.wait()

    def bias2(c):
        bias = bias_ref[0, c]
        return jnp.concatenate([bias, bias], axis=0)

    def stage_scores(hp, s_list):
        for j, s in enumerate(s_list):
            s_ref[hp, j] = s
        return jnp.max(_tree(jnp.maximum, s_list), axis=-1, keepdims=True)

    def apply_update(hp, n, cmax, pv_of):
        m = m_ref[hp]
        m_new = jnp.maximum(m, cmax)
        alpha = jnp.exp(m - m_new)
        ps = [jnp.exp(s_ref[hp, j] - m_new) for j in range(n)]
        l_ref[hp] = alpha * l_ref[hp] + jnp.sum(_tree(lambda a, b: a + b, ps), axis=-1, keepdims=True)
        acc_ref[hp] = alpha * acc_ref[hp] + _tree(lambda a, b: a + b,
                                                  [pv_of(j, p.astype(BF16)) for j, p in enumerate(ps)])
        m_ref[hp] = m_new

    pairs = [slice(hp * LANES, (hp + 1) * LANES) for hp in range(NPAIR)]
    biases = [bias2(g * gp + j) for j in range(gp)]
    cmax = [stage_scores(hp, [_dot(qs_ref[hp].astype(BF16), kbuf_ref[slot, j, pairs[hp], :].astype(BF16)) + biases[j]
                              for j in range(gp)]) for hp in range(NPAIR)]
    for hp in range(NPAIR):
        apply_update(hp, gp, cmax[hp],
                     lambda j, p, hp=hp: _dot_nt(p, vbuf_ref[slot, j, pairs[hp], :].astype(BF16)))

    @pl.when(g == ng - 1)
    def _():
        knew_ref[...] = jnp.zeros(knew_ref.shape, F32)
        vnew_ref[...] = jnp.zeros(vnew_ref.shape, F32)
        knew_ref[0:tq, :] = kn_ref[0]
        vnew_ref[0:tq, :] = vn_ref[0]
        bnew = bias2(npages)
        for hp in range(NPAIR):
            s = _dot_nt(qs_ref[hp].astype(BF16), knew_ref[:, pairs[hp]].astype(BF16)) + bnew
            apply_update(hp, 1, stage_scores(hp, [s]),
                         lambda j, p, hp=hp: _dot(p, vnew_ref[:, pairs[hp]].astype(BF16)))
            o_ref[0, :, pairs[hp]] = _merge_pair(acc_ref[hp] / l_ref[hp], lane, tq)


def _attn_sample(page_table, q, k_new, v_new, bias, ck_t, cv_t, gp):
    b, tq, _ = q.shape
    npages = page_table.shape[1]
    nseq = bias.shape[2] // tq
    assert npages % gp == 0 and bias.shape[1] == npages + 1 and tq <= PAGE
    blk = lambda n: pl.BlockSpec((1, tq, n), lambda bi, gi, pt: (bi, 0, 0))
    grid_spec = pltpu.PrefetchScalarGridSpec(
        num_scalar_prefetch=1,
        grid=(b, npages // gp),
        in_specs=[blk(ATT_WIDTH), blk(ATT_WIDTH), blk(ATT_WIDTH),
                  pl.BlockSpec((1, npages + 1, tq, PAGE), lambda bi, gi, pt: (bi // nseq, 0, bi % nseq, 0)),
                  pl.BlockSpec(memory_space=pl.ANY), pl.BlockSpec(memory_space=pl.ANY)],
        out_specs=blk(ATT_WIDTH),
        scratch_shapes=[pltpu.VMEM((KV_SLOTS, gp, ATT_WIDTH, PAGE), F32),
                        pltpu.VMEM((KV_SLOTS, gp, ATT_WIDTH, PAGE), F32),
                        pltpu.VMEM((NPAIR, 2 * tq, LANES), F32),
                        pltpu.VMEM((PAGE, ATT_WIDTH), F32),
                        pltpu.VMEM((PAGE, ATT_WIDTH), F32),
                        pltpu.VMEM((NPAIR, 2 * tq, 1), F32),
                        pltpu.VMEM((NPAIR, 2 * tq, 1), F32),
                        pltpu.VMEM((NPAIR, 2 * tq, LANES), F32),
                        pltpu.VMEM((NPAIR, gp, 2 * tq, PAGE), F32),
                        pltpu.SemaphoreType.DMA((2, KV_SLOTS))],
    )
    return pl.pallas_call(
        functools.partial(_attn_sample_kernel, tq=tq, gp=gp, npages=npages, nb=b),
        grid_spec=grid_spec,
        out_shape=jax.ShapeDtypeStruct((b, tq, ATT_WIDTH), F32),
        compiler_params=_cparams(("arbitrary", "arbitrary")),
        name="attn_sample",
    )(page_table, q, k_new, v_new, bias, ck_t, cv_t)


def _cumsum_rows(x):
    n = x.shape[0]
    row = lax.broadcasted_iota(I32, x.shape, 0)
    sh = 1
    while sh < n:
        x = x + jnp.where(row >= sh, pltpu.roll(x, sh, 0), 0.0)
        sh *= 2
    return x


def _hgrn_kernel(hq_ref, lf_ref, hk_ref, hv_ref, hg_ref, s0_ref, ng_ref, o_ref, sout_ref,
                 st_ref, cum_ref, *, tb, c, sb):
    t = pl.program_id(2)
    nt = pl.num_programs(2)

    @pl.when(t == 0)
    def _():
        st_ref[...] = s0_ref[0, 0].T

    row8 = lax.broadcasted_iota(I32, (SUBLANES, 1), 0)
    row_c = lax.broadcasted_iota(I32, (c, 1), 0)
    nsb = c // sb

    for j in range(tb // c):
        rows = slice(j * c, (j + 1) * c)
        lf = lf_ref[0, rows, :]
        q = hq_ref[0, rows, :]
        k = hk_ref[0, rows, :]
        v = hv_ref[0, rows, :]
        cum = _cumsum_rows(lf) * LOG2E
        cum_ref[...] = cum
        last = cum[c - 1:c, :]
        st = st_ref[...]
        o = _dot_nt((q * jnp.exp2(cum)).astype(BF16), st.astype(BF16))

        vb = v.astype(BF16)
        o_parts = []
        a_rows = []
        for i in range(nsb):
            base = i * sb
            groups = [slice(base + r, base + r + SUBLANES) for r in range(0, sb, SUBLANES)]
            od = [jnp.zeros((SUBLANES, HG_DV), F32) for _ in groups]
            for s in range(sb):
                sr = base + s
                row = slice(j * c + sr, j * c + sr + 1)
                cum_s, k_s, v_s = cum_ref[sr:sr + 1, :], hk_ref[0, row, :], hv_ref[0, row, :]
                for gi, rr in enumerate(groups):
                    first = gi * SUBLANES
                    if first + SUBLANES <= s:
                        continue
                    ex = cum[rr] - cum_s
                    if first <= s:
                        ex = jnp.where(row8 >= s - first, ex, NEG_INF)
                    a_col = jnp.sum(jnp.exp2(ex) * (q[rr] * k_s), axis=-1, keepdims=True)
                    od[gi] = od[gi] + a_col * v_s
            o_parts.extend(od)
            if i > 0:
                c0 = cum[base:base + 1, :] - lf[base:base + 1, :] * LOG2E
                qs = (q[base:base + sb] * jnp.exp2(cum[base:base + sb] - c0)).astype(BF16)
                ks = (k * jnp.exp2(jnp.where(row_c < base, c0 - cum, NEG_INF))).astype(BF16)
                a_rows.append(_dot_nt(qs, ks))
            elif nsb > 1:
                a_rows.append(jnp.zeros((sb, c), F32))
        o = o + jnp.concatenate(o_parts, axis=0)
        if nsb > 1:
            o = o + _dot(jnp.concatenate(a_rows, axis=0).astype(BF16), vb)

        kd = (k * jnp.exp2(last - cum)).astype(BF16)
        st_ref[...] = st * jnp.exp2(last) + _dot_tn(vb, kd)

        ms = jnp.mean(o * o, axis=-1, keepdims=True)
        hg = hg_ref[0, rows, :]
        o_ref[0, rows, :] = (((o * lax.rsqrt(ms + RMS_EPS)) * ng_ref[...]) * (hg * _sigmoid(hg))).astype(BF16)

    @pl.when(t == nt - 1)
    def _():
        sout_ref[0, 0] = st_ref[...].T


def _hgrn(hq, lf, hk, hv, hg, s0, norm_g, tb, c):
    b, t, _ = hq.shape
    sb = 16
    assert t % tb == 0 and tb % c == 0 and c % sb == 0
    blk = pl.BlockSpec((1, tb, HG_DK), lambda bi, h, ti: (bi, ti, h))
    sblk = pl.BlockSpec((1, 1, HG_DK, HG_DV), lambda bi, h, ti: (bi, h, 0, 0))
    return pl.pallas_call(
        functools.partial(_hgrn_kernel, tb=tb, c=c, sb=sb),
        grid=(b, HG_HEADS, t // tb),
        in_specs=[blk, blk, blk, blk, blk, sblk, pl.BlockSpec((1, HG_DV), lambda bi, h, ti: (0, 0))],
        out_specs=[blk, sblk],
        out_shape=[jax.ShapeDtypeStruct((b, t, HG_WIDTH), BF16),
                   jax.ShapeDtypeStruct((b, HG_HEADS, HG_DK, HG_DV), F32)],
        scratch_shapes=[pltpu.VMEM((HG_DV, HG_DK), F32), pltpu.VMEM((c, HG_DK), F32)],
        compiler_params=_cparams(("arbitrary", "arbitrary", "arbitrary")),
        name="hgrn",
    )(hq, lf, hk, hv, hg, s0, norm_g)


def _merge_kernel(x_ref, att_ref, ga_ref, hb_ref, ma_ref, mb_ref, p_ref,
                  wa_ref, wb_ref, wo_ref, wpg_ref, wpp_ref, gp_ref, o_ref):
    ga = ga_ref[...]
    ya = _dot((att_ref[...] * (ga * _sigmoid(ga))).astype(BF16), wa_ref[...])
    yb = _dot(hb_ref[...], wb_ref[...])
    mixed = _sigmoid(ma_ref[...]) * ya + _sigmoid(mb_ref[...]) * yb
    u = _dot(mixed.astype(BF16), wo_ref[...])
    ms = jnp.mean(u * u, axis=-1, keepdims=True)
    h1 = x_ref[...] + (u * lax.rsqrt(ms + RMS_EPS)) * gp_ref[...]
    gate = _sigmoid(_dot(h1.astype(BF16), wpg_ref[...]))
    o_ref[...] = h1 + gate * _dot(p_ref[...].astype(BF16), wpp_ref[...])


def _merge(x2, att, ga, hb, ma, mb, p2, w_a, w_b, w_o, w_pg, w_pp, g_post, tm):
    m, d = x2.shape
    assert m % tm == 0
    row = lambda a: pl.BlockSpec((tm, a.shape[1]), lambda i: (i, 0))
    full = lambda a: pl.BlockSpec(a.shape, lambda i: (0, 0))
    acts = (x2, att, ga, hb, ma, mb, p2)
    wts = (w_a, w_b, w_o, w_pg, w_pp, g_post)
    return pl.pallas_call(
        _merge_kernel,
        grid=(m // tm,),
        in_specs=[row(a) for a in acts] + [full(w) for w in wts],
        out_specs=pl.BlockSpec((tm, d), lambda i: (i, 0)),
        out_shape=jax.ShapeDtypeStruct((m, d), F32),
        compiler_params=_cparams(("arbitrary",)),
        name="merge",
    )(*acts, *wts)


def _pick_rows(m, pref):
    return pref if m % pref == 0 else m


def kernel(x_prompt, x_sample, p_prompt, p_sample, cache_k, cache_v, cache_kidx, state_hgrn, page_table,
           g_pre, g_post, w_in, hgrn_lb_logits, hgrn_norm_g, w_branch_a, w_branch_b, w_out,
           w_ple_gate, w_ple_proj):
    depth = w_in.shape[0]
    bp, tp, d = x_prompt.shape
    bs, ts, _ = x_sample.shape
    nphys = cache_k.shape[1]
    hp, hs = x_prompt, x_sample
    outs = [[] for _ in range(8)]
    w = ATT_WIDTH
    a_end = 4 * w + IDX_WIDTH
    b_end = a_end + IDX_DIM + IDX_HEADS

    for i in range(depth):
        wi_ = w_in[i].astype(BF16)
        wa, wb, wc = wi_[:, :a_end], wi_[:, a_end:b_end], wi_[:, b_end:]
        wt = jnp.concatenate([wi_[:, 0:3 * w], wi_[:, 4 * w:b_end]], axis=1).T
        wtok = jnp.concatenate([wi_[:, w:2 * w], wi_[:, 3 * w:4 * w], wc], axis=1)
        g_pre_i = g_pre[i].reshape(1, d)
        wts = (w_branch_a[i].astype(BF16), w_branch_b[i].astype(BF16), w_out[i].astype(BF16),
               w_ple_gate[i].astype(BF16), w_ple_proj[i].astype(BF16), g_post[i].reshape(1, d))
        norm_g = hgrn_norm_g[i].reshape(1, HG_DV)

        mp = bp * tp
        kc = _pick_rows(tp, 256)
        (qt, kt, ktok, vt, vtc, ga, qit, kit, kitok, wit, hq, lf, hk, hv, hg, ma, mb) = _inproj_prompt(
            hp.reshape(mp, d), g_pre_i, hgrn_lb_logits, wt, wtok, wb, i, kc, bp)
        r3 = lambda a, b_=bp, t_=tp: a.reshape(b_, t_, a.shape[-1])
        att = _attn_prompt(qit, wit, kitok, qt, ktok, vtc, qb=_pick_rows(tp, 256))
        s0 = jnp.zeros((bp, HG_HEADS, HG_DK, HG_DV), F32)
        hb, s_p = _hgrn(r3(hq), r3(lf), r3(hk), r3(hv), r3(hg), s0, norm_g, tb=_pick_rows(tp, 256), c=64)
        y = _merge(hp.reshape(mp, d), att.reshape(mp, w), ga, hb.reshape(mp, HG_WIDTH), ma, mb,
                   p_prompt[i].reshape(mp, -1), *wts, _pick_rows(mp, 256))
        hp = y.reshape(bp, tp, d)
        outs[0].append(kt.reshape(bp, N_HEADS, HEAD_DIM, tp).transpose(0, 3, 1, 2))
        outs[1].append(vt.reshape(bp, N_HEADS, HEAD_DIM, tp).transpose(0, 3, 1, 2))
        outs[2].append(kit.transpose(0, 2, 1))
        outs[3].append(s_p.astype(state_hgrn.dtype))

        ms = bs * ts
        (q, k, v, ga, qi, ki, wi, hq, lf, hk, hv, hg, ma, mb) = _inproj_sample(
            hs.reshape(ms, d), g_pre_i, hgrn_lb_logits, wa, wb, wc, i, _pick_rows(ms, 256))
        r3 = lambda a, b_=bs, t_=ts: a.reshape(b_, t_, a.shape[-1])
        cki_t = cache_kidx[i].transpose(0, 2, 1)
        ck_t = cache_k[i].transpose(0, 2, 3, 1).reshape(nphys, w, PAGE)
        cv_t = cache_v[i].transpose(0, 2, 3, 1).reshape(nphys, w, PAGE)
        bias = _index_sample(page_table, r3(qi), r3(wi), r3(ki), cki_t, nseq=4 if bs % 4 == 0 else 1)
        att = _attn_sample(page_table, r3(q), r3(k), r3(v), bias, ck_t, cv_t, gp=8)
        tpad = -ts % 16
        pad = lambda a: jnp.pad(r3(a), ((0, 0), (0, tpad), (0, 0)))
        hb, s_s = _hgrn(pad(hq), pad(lf), pad(hk), pad(hv), pad(hg), state_hgrn[i], norm_g,
                        tb=ts + tpad, c=ts + tpad)
        y = _merge(hs.reshape(ms, d), att.reshape(ms, w), ga, hb[:, :ts].reshape(ms, HG_WIDTH), ma, mb,
                   p_sample[i].reshape(ms, -1), *wts, _pick_rows(ms, 256))
        hs = y.reshape(bs, ts, d)
        outs[4].append(k.reshape(bs, ts, N_HEADS, HEAD_DIM))
        outs[5].append(v.reshape(bs, ts, N_HEADS, HEAD_DIM))
        outs[6].append(ki.reshape(bs, ts, IDX_DIM))
        outs[7].append(s_s.astype(state_hgrn.dtype))

    st = [jnp.stack(o) for o in outs]
    return (hp, hs, st[0], st[1], st[2], st[3], st[4], st[5], st[6], st[7])
```

```python
import functools

import jax
import jax.numpy as jnp
from jax import lax
from jax.experimental import pallas as pl
from jax.experimental.pallas import tpu as pltpu

F32 = jnp.float32
BF16 = jnp.bfloat16
I32 = jnp.int32

N_HEADS = 8
HEAD_DIM = 64
ATT_WIDTH = N_HEADS * HEAD_DIM
IDX_HEADS = 8
IDX_DIM = 64
IDX_WIDTH = IDX_HEADS * IDX_DIM
TOPK_MAX = 256
HG_HEADS = 4
HG_DK = 128
HG_DV = 128
HG_WIDTH = HG_HEADS * HG_DV
PAGE = 128
RMS_EPS = 1e-6

LANES = 128
SUBLANES = 8
BF16_ROWS = 16
VMEM_LIMIT = 56 * 1024 * 1024
NEG_INF = float("-inf")
M_INIT = -1e30
INT_MAG = 0x7FFFFFFF
NPAIR = N_HEADS // 2
LOG2E = 1.4426950408889634
FLOAT_BISECT_STEPS = 12
UNTESTED_PASSES = 9
PASSES_PER_TEST = 3
KV_SLOTS = 3


def _cparams(sem):
    return pltpu.CompilerParams(dimension_semantics=sem, vmem_limit_bytes=VMEM_LIMIT)


def _sigmoid(x):
    return 1.0 / (1.0 + jnp.exp(-x))


def _dot(a, b):
    return jnp.dot(a, b, preferred_element_type=F32)


def _dot_nt(a, b):
    return lax.dot_general(a, b, (((1,), (1,)), ((), ())), preferred_element_type=F32)


def _dot_tn(a, b):
    return lax.dot_general(a, b, (((0,), (0,)), ((), ())), preferred_element_type=F32)


def _tree(op, xs):
    xs = list(xs)
    while len(xs) > 1:
        xs = [op(xs[i], xs[i + 1]) if i + 1 < len(xs) else xs[i] for i in range(0, len(xs), 2)]
    return xs[0]


def _fold_rows(x, op):
    return _tree(op, [x[r:r + SUBLANES] for r in range(0, x.shape[0], SUBLANES)])


def _two_stage(n, stage_a, stage_b, carry):
    def two(j, st):
        a0, carry = st
        a1 = stage_a(2 * j + 1, 1)
        carry = stage_b(2 * j, 0, a0, carry)
        a0 = stage_a(2 * j + 2, 0)
        carry = stage_b(2 * j + 1, 1, a1, carry)
        return a0, carry

    npairs = (n - 1) // 2
    a0, carry = lax.fori_loop(0, npairs, two, (stage_a(0, 0), carry))
    last = 2 * npairs

    def tail2():
        a1 = stage_a(last + 1, 1)
        return stage_b(last + 1, 1, a1, stage_b(last, 0, a0, carry))

    return lax.cond(last + 1 < n, tail2, lambda: stage_b(last, 0, a0, carry))


def _chunk_loop2(n, body, carry):
    carry = lax.fori_loop(0, n // 2, lambda j, cr: body(2 * j + 1, body(2 * j, cr)), carry)
    return lax.cond(n % 2 == 1, lambda: body(n - 1, carry), lambda: carry)


def _normed(x_ref, g_ref):
    x = x_ref[...]
    var = jnp.mean(x * x, axis=-1, keepdims=True)
    return ((x * lax.rsqrt(var + RMS_EPS)) * g_ref[...]).astype(BF16)


def _hgrn_and_merge_gates(xn, w_ref, off, lbl_ref, layer, hq_ref, lf_ref, hk_ref, hv_ref, hg_ref, ma_ref, mb_ref):
    def proj(lo, hi):
        return _dot(xn, w_ref[:, off + lo:off + hi])

    lg = lbl_ref[...]
    e = jnp.exp(lg - jnp.max(lg, axis=0, keepdims=True))
    sm = e / jnp.sum(e, axis=0, keepdims=True)
    lb = jnp.sum(sm[:layer + 1], axis=0, keepdims=True)
    one_m = 1.0 - lb

    hw = HG_WIDTH
    hq = proj(0, hw)
    hq_ref[...] = hq * _sigmoid(hq)
    zf = proj(hw, 2 * hw)
    lf_ref[...] = jnp.log(lb + one_m * _sigmoid(zf))
    hk_ref[...] = one_m * _sigmoid(-zf)
    hv_ref[...] = proj(2 * hw, 3 * hw)
    hg_ref[...] = proj(3 * hw, 4 * hw)
    d = ma_ref.shape[-1]
    ma_ref[...] = proj(4 * hw, 4 * hw + d)
    mb_ref[...] = proj(4 * hw + d, 4 * hw + 2 * d)


def _inproj_sample_kernel(x_ref, g_ref, lbl_ref, wa_ref, wb_ref, wc_ref,
                          q_ref, k_ref, v_ref, ga_ref, qi_ref, ki_ref, wi_ref, *tail, layer):
    xn = _normed(x_ref, g_ref)
    w = ATT_WIDTH
    q_ref[...] = (_dot(xn, wa_ref[:, 0:w]) * (HEAD_DIM ** -0.5)).astype(BF16)
    k_ref[...] = _dot(xn, wa_ref[:, w:2 * w])
    v_ref[...] = _dot(xn, wa_ref[:, 2 * w:3 * w])
    ga_ref[...] = _dot(xn, wa_ref[:, 3 * w:4 * w])
    qi_ref[...] = (_dot(xn, wa_ref[:, 4 * w:4 * w + IDX_WIDTH]) * (IDX_DIM ** -0.5)).astype(BF16)
    kiwi = _dot(xn, wb_ref[...])
    ki_ref[...] = kiwi[:, :IDX_DIM]
    wi_ref[...] = kiwi[:, IDX_DIM:IDX_DIM + IDX_HEADS] * (IDX_HEADS ** -0.5)
    _hgrn_and_merge_gates(xn, wc_ref, 0, lbl_ref, layer, *tail)


def _inproj_prompt_kernel(x_ref, g_ref, lbl_ref, wt_ref, wtok_ref, wb_ref,
                          qt_ref, kt_ref, ktok_ref, vt_ref, vtc_ref, ga_ref, qit_ref, kit_ref, kitok_ref, wit_ref,
                          *tail, layer):
    xn = _normed(x_ref, g_ref)
    w = ATT_WIDTH

    def tproj(lo, hi):
        return _dot_nt(wt_ref[lo:hi, :], xn)

    qt_ref[0] = (tproj(0, w) * (HEAD_DIM ** -0.5 * LOG2E)).astype(BF16)
    kt_ref[0] = tproj(w, 2 * w)
    vt = tproj(2 * w, 3 * w)
    vt_ref[0] = vt
    vtc_ref[0, 0] = vt.astype(BF16)
    o = 3 * w
    qit_ref[0] = (tproj(o, o + IDX_WIDTH) * (IDX_DIM ** -0.5)).astype(BF16)
    o += IDX_WIDTH
    kit_ref[0] = tproj(o, o + IDX_DIM)
    wit_ref[0] = tproj(o + IDX_DIM, o + IDX_DIM + IDX_HEADS) * (IDX_HEADS ** -0.5)

    ktok_ref[0, 0] = _dot(xn, wtok_ref[:, 0:w]).astype(BF16)
    kitok_ref[0, 0] = _dot(xn, wb_ref[:, 0:IDX_DIM]).astype(BF16)
    ga_ref[...] = _dot(xn, wtok_ref[:, w:2 * w])
    _hgrn_and_merge_gates(xn, wtok_ref, 2 * w, lbl_ref, layer, *tail)


def _inproj_sample(x2, g_pre, lb_logits, wa, wb, wc, layer, tm):
    m, d = x2.shape
    assert m % tm == 0
    row = lambda n, dt: (pl.BlockSpec((tm, n), lambda i: (i, 0)), jax.ShapeDtypeStruct((m, n), dt))
    full = lambda a: pl.BlockSpec(a.shape, lambda i: (0,) * a.ndim)
    outs = [row(ATT_WIDTH, BF16), row(ATT_WIDTH, F32), row(ATT_WIDTH, F32), row(ATT_WIDTH, F32),
            row(IDX_WIDTH, BF16), row(IDX_DIM, F32), row(IDX_HEADS, F32)] + [row(HG_WIDTH, F32)] * 5 + [row(d, F32)] * 2
    ins = [x2, g_pre, lb_logits, wa, wb, wc]
    return pl.pallas_call(
        functools.partial(_inproj_sample_kernel, layer=layer),
        grid=(m // tm,),
        in_specs=[pl.BlockSpec((tm, d), lambda i: (i, 0))] + [full(a) for a in ins[1:]],
        out_specs=[o[0] for o in outs],
        out_shape=[o[1] for o in outs],
        compiler_params=_cparams(("arbitrary",)),
        name="inproj_sample",
    )(*ins)


def _inproj_prompt(x2, g_pre, lb_logits, wt, wtok, wb, layer, tm, batch):
    m, d = x2.shape
    t = m // batch
    assert m % tm == 0 and t % tm == 0
    nt = t // tm
    row = lambda n, dt: (pl.BlockSpec((tm, n), lambda i: (i, 0)), jax.ShapeDtypeStruct((m, n), dt))
    full = lambda a: pl.BlockSpec(a.shape, lambda i: (0,) * a.ndim)
    fmaj = lambda n, dt: (pl.BlockSpec((1, n, tm), lambda i: (i // nt, 0, i % nt)),
                          jax.ShapeDtypeStruct((batch, n, t), dt))
    fchunk = lambda n: (pl.BlockSpec((1, 1, n, tm), lambda i: (i // nt, i % nt, 0, 0)),
                        jax.ShapeDtypeStruct((batch, nt, n, tm), BF16))
    tchunk = lambda n: (pl.BlockSpec((1, 1, tm, n), lambda i: (i // nt, i % nt, 0, 0)),
                        jax.ShapeDtypeStruct((batch, nt, tm, n), BF16))
    outs = [fmaj(ATT_WIDTH, BF16), fmaj(ATT_WIDTH, F32), tchunk(ATT_WIDTH), fmaj(ATT_WIDTH, F32), fchunk(ATT_WIDTH),
            row(ATT_WIDTH, F32), fmaj(IDX_WIDTH, BF16), fmaj(IDX_DIM, F32), tchunk(IDX_DIM), fmaj(IDX_HEADS, F32)]
    outs += [row(HG_WIDTH, F32)] * 5 + [row(d, F32)] * 2
    ins = [x2, g_pre, lb_logits, wt, wtok, wb]
    return pl.pallas_call(
        functools.partial(_inproj_prompt_kernel, layer=layer),
        grid=(m // tm,),
        in_specs=[pl.BlockSpec((tm, d), lambda i: (i, 0))] + [full(a) for a in ins[1:]],
        out_specs=[o[0] for o in outs],
        out_shape=[o[1] for o in outs],
        compiler_params=_cparams(("arbitrary",)),
        name="inproj_prompt",
    )(*ins)


def _f2key(x):
    b = lax.bitcast_convert_type(x, I32)
    return b ^ ((b >> 31) & INT_MAG)


def _key2f(k):
    return lax.bitcast_convert_type(k ^ ((k >> 31) & INT_MAG), F32)


def _select_topk(count, rewrite, top, n_valid, mx, mn, last_pos, index_steps):
    topf = float(top)

    small = n_valid <= top
    lo0 = _f2key(mn)
    hi0 = _f2key(mx) + 1
    conv0 = jnp.logical_and(jnp.logical_not(small), lo0 + 1 >= hi0)
    done0 = jnp.logical_or(small, conv0)
    thr0 = mn
    ghi0 = jnp.zeros(mn.shape, F32)

    def zero_counts():
        return count(lambda x, kpos: x > 0.0), count(lambda x, kpos: x >= 0.0)

    n_pos, n_nonneg = lax.cond(jnp.sum(1.0 - done0.astype(F32)) > 0.0, zero_counts, lambda: (ghi0, ghi0))
    zero_tie = jnp.logical_and(jnp.logical_not(done0), jnp.logical_and(n_pos < topf, n_nonneg >= topf))
    thr0 = jnp.where(zero_tie, 0.0, thr0)
    ghi0 = jnp.where(zero_tie, n_pos, ghi0)
    tie0 = jnp.logical_or(conv0, zero_tie)
    done0 = jnp.logical_or(done0, zero_tie)
    n_live0 = jnp.sum(1.0 - done0.astype(F32))
    state0 = (lo0, hi0, ghi0, thr0, done0.astype(I32), tie0.astype(I32), jnp.int32(0))

    def one_pass(st):
        lo, hi, ghi, thr, done_i, tie_i, it = st
        done = done_i > 0
        lof, hif = _key2f(lo), _key2f(hi)
        midf = 0.5 * lof + 0.5 * hif
        use_f = jnp.logical_and(it < FLOAT_BISECT_STEPS, jnp.logical_and(midf > lof, midf < hif))
        mid = jnp.where(use_f, _f2key(midf), (lo & hi) + ((lo ^ hi) >> 1))
        tmid = _key2f(mid)
        g = count(lambda x, kpos: x >= tmid)
        live = jnp.logical_not(done)
        is_eq = jnp.logical_and(live, g == topf)
        up = jnp.logical_and(live, g > topf)
        down = jnp.logical_and(live, g < topf)
        lo = jnp.where(up, mid, lo)
        hi = jnp.where(down, mid, hi)
        ghi = jnp.where(down, g, ghi)
        thr = jnp.where(is_eq, tmid, thr)
        done = jnp.logical_or(done, is_eq)
        conv = jnp.logical_and(jnp.logical_not(done), lo + 1 >= hi)
        thr = jnp.where(conv, _key2f(lo), thr)
        done = jnp.logical_or(done, conv)
        tie_i = jnp.where(conv, 1, tie_i)
        return (lo, hi, ghi, thr, done.astype(I32), tie_i, it + 1)

    def passes(carry):
        st = carry[0]
        for _ in range(PASSES_PER_TEST):
            st = one_pass(st)
        return st, jnp.sum(1.0 - st[4].astype(F32))

    state = lax.cond(n_live0 > 0.0,
                     lambda: lax.fori_loop(0, UNTESTED_PASSES, lambda _, st: one_pass(st), state0),
                     lambda: state0)
    (_, _, ghi, thr, _, tie_i, _), _ = lax.while_loop(lambda carry: carry[1] > 0.0, passes, (state, n_live0))

    tie = tie_i > 0
    n_tie = jnp.sum(tie_i.astype(F32))

    @pl.when(n_tie > 0.0)
    def _():
        need = topf - ghi
        jlo0 = jnp.full(mn.shape, -1, I32)
        jhi0 = jnp.full(mn.shape, 0, I32) + last_pos

        def jbody(_, carry):
            jlo, jhi = carry
            jm = (jlo + jhi) >> 1
            cnt = count(lambda x, kpos: jnp.logical_and(x == thr, kpos <= jm))
            ok = cnt >= need
            return jnp.where(ok, jlo, jm), jnp.where(ok, jm, jhi)

        _, jcut = lax.fori_loop(0, index_steps, jbody, (jlo0, jhi0))
        rewrite(lambda x, kpos: jnp.where(
            jnp.logical_and(jnp.logical_and(tie, x == thr), kpos > jcut), NEG_INF, x))

    rewrite(lambda x, kpos: jnp.where(x >= thr, 0.0, NEG_INF))


def _index_steps(n_positions):
    return max(1, (n_positions - 1).bit_length()) + 1


def _attn_prompt_kernel(qit_ref, wit_ref, kitok_ref, qt_ref, ktok_ref, vt_ref, o_ref,
                        sc_ref, qis_ref, qs_ref, m_ref, acc_ref, s_ref, d_ref, *, qb, kc, top):
    i = pl.program_id(1)
    nkc = ((i + 1) * qb + kc - 1) // kc

    for h in range(IDX_HEADS):
        qis_ref[:, h * qb:(h + 1) * qb] = qit_ref[0, h * IDX_DIM:(h + 1) * IDX_DIM, :]
    feat = lax.broadcasted_iota(I32, (LANES, qb), 0)
    for hp in range(NPAIR):
        qp = qt_ref[0, hp * LANES:(hp + 1) * LANES, :].astype(F32)
        qs_ref[hp] = jnp.concatenate([jnp.where(feat < HEAD_DIM, qp, 0.0),
                                      jnp.where(feat >= HEAD_DIM, qp, 0.0)], axis=1).astype(BF16)

    wit = wit_ref[0]
    wrows = [wit[h:h + 1, :] for h in range(IDX_HEADS)]
    qpos = i * qb + lax.broadcasted_iota(I32, (1, qb), 1)
    krow = lax.broadcasted_iota(I32, (kc, 1), 0)

    def kpos_of(c):
        return c * kc + krow

    half = IDX_HEADS * qb // 2

    def score_a(c, slot):
        kic = kitok_ref[0, c]
        d_ref[slot, :, 0:half] = _dot(kic, qis_ref[:, 0:half])
        d_ref[slot, :, half:2 * half] = _dot(kic, qis_ref[:, half:2 * half])
        return ()

    def score_b(c, slot, _, carry):
        mxp, mnp = carry
        acc = _tree(lambda a, b: a + b,
                    [wrows[h] * jnp.maximum(d_ref[slot, :, h * qb:(h + 1) * qb], 0.0) for h in range(IDX_HEADS)])
        valid = kpos_of(c) <= qpos
        x = jnp.where(valid, acc, NEG_INF)
        sc_ref[c] = x
        mxp = jnp.maximum(mxp, _fold_rows(x, jnp.maximum))
        mnp = jnp.minimum(mnp, _fold_rows(jnp.where(valid, acc, -NEG_INF), jnp.minimum))
        return mxp, mnp

    mxp, mnp = _two_stage(nkc, score_a, score_b, (jnp.full((SUBLANES, qb), NEG_INF, F32),
                                                  jnp.full((SUBLANES, qb), -NEG_INF, F32)))
    mx = jnp.max(mxp, axis=0, keepdims=True)
    mn = jnp.min(mnp, axis=0, keepdims=True)

    def count(pred):
        def body(c, acc):
            return acc + _fold_rows(jnp.where(pred(sc_ref[c], kpos_of(c)), 1.0, 0.0), lambda a, b: a + b)
        return jnp.sum(_chunk_loop2(nkc, body, jnp.zeros((SUBLANES, qb), F32)), axis=0, keepdims=True)

    def rewrite(fn):
        def body(c, _):
            sc_ref[c] = fn(sc_ref[c], kpos_of(c))
            return 0
        lax.fori_loop(0, nkc, body, 0)

    _select_topk(count, rewrite, top, qpos + 1, mx, mn, nkc * kc - 1, _index_steps(sc_ref.shape[0] * kc))

    m_ref[...] = jnp.full(m_ref.shape, M_INIT, F32)
    acc_ref[...] = jnp.zeros(acc_ref.shape, F32)
    ones = jnp.ones((BF16_ROWS, kc), BF16)

    def att_a(c, slot):
        bias = sc_ref[c]
        bias2 = jnp.concatenate([bias, bias], axis=1)
        cmax = []
        for hp in range(NPAIR):
            s = _dot(ktok_ref[0, c, :, hp * LANES:(hp + 1) * LANES], qs_ref[hp]) + bias2
            s_ref[slot, hp] = s
            cmax.append(jnp.max(_fold_rows(s, jnp.maximum), axis=0, keepdims=True))
        return cmax

    def att_b(c, slot, cmax, carry):
        for hp in range(NPAIR):
            m = m_ref[hp]
            m_new = jnp.maximum(m, cmax[hp])
            p = jnp.exp2(s_ref[slot, hp] - m_new).astype(BF16)
            v1 = jnp.concatenate([vt_ref[0, c, hp * LANES:(hp + 1) * LANES, :], ones], axis=0)
            acc_ref[hp] = jnp.exp2(m - m_new) * acc_ref[hp] + _dot(v1, p)
            m_ref[hp] = m_new
        return carry

    _two_stage(nkc, att_a, att_b, 0)

    for hp in range(NPAIR):
        acc = acc_ref[hp]
        out = acc[0:LANES] / acc[LANES:LANES + 1]
        out = jnp.where(feat < HEAD_DIM, out[:, 0:qb], out[:, qb:2 * qb])
        o_ref[0, :, hp * LANES:(hp + 1) * LANES] = out.T


def _attn_prompt(qit, wit, kitok, qt, ktok, vtc, qb):
    b, _, t = qt.shape
    nc, kc = ktok.shape[1], ktok.shape[2]
    top = min(TOPK_MAX, t // 4)
    assert t % qb == 0 and nc * kc == t and kc % qb == 0 and qb % LANES == 0
    qblk = lambda n: pl.BlockSpec((1, n, qb), lambda bi, i: (bi, 0, i))
    whole = lambda a: pl.BlockSpec((1,) + a.shape[1:], lambda bi, i: (bi, 0, 0, 0), pipeline_mode=pl.Buffered(1))
    return pl.pallas_call(
        functools.partial(_attn_prompt_kernel, qb=qb, kc=kc, top=top),
        grid=(b, t // qb),
        in_specs=[qblk(IDX_WIDTH), qblk(IDX_HEADS), whole(kitok), qblk(ATT_WIDTH), whole(ktok), whole(vtc)],
        out_specs=pl.BlockSpec((1, qb, ATT_WIDTH), lambda bi, i: (bi, i, 0)),
        out_shape=jax.ShapeDtypeStruct((b, t, ATT_WIDTH), F32),
        scratch_shapes=[pltpu.VMEM((nc, kc, qb), F32),
                        pltpu.VMEM((IDX_DIM, IDX_HEADS * qb), BF16),
                        pltpu.VMEM((NPAIR, LANES, 2 * qb), BF16),
                        pltpu.VMEM((NPAIR, 1, 2 * qb), F32),
                        pltpu.VMEM((NPAIR, LANES + BF16_ROWS, 2 * qb), F32),
                        pltpu.VMEM((2, NPAIR, kc, 2 * qb), F32),
                        pltpu.VMEM((2, kc, IDX_HEADS * qb), F32)],
        compiler_params=_cparams(("arbitrary", "arbitrary")),
        name="attn_prompt",
    )(qit, wit, kitok, qt, ktok, vtc)


def _split_pair(qp, lane):
    return jnp.concatenate([jnp.where(lane < HEAD_DIM, qp, 0.0), jnp.where(lane >= HEAD_DIM, qp, 0.0)], axis=0)


def _merge_pair(x, lane, rows):
    return jnp.where(lane < HEAD_DIM, x[0:rows], x[rows:2 * rows])


def _indexer_scores(d, wfull, rows):
    return _tree(lambda a, b: a + b,
                 [wfull[h] * jnp.maximum(d[h * rows:(h + 1) * rows], 0.0) for h in range(IDX_HEADS)])


def _index_sample_kernel(pt_ref, qi_ref, wi_ref, kin_ref, cki_hbm, bias_ref,
                         kibuf_ref, qis_ref, knew_ref, sem, *, nseq, tq, npages, top, unroll):
    b0 = pl.program_id(0) * nseq
    past = npages * PAGE
    nkc = npages + 1

    def page_copy(sq, p):
        return pltpu.make_async_copy(cki_hbm.at[pt_ref[b0 + sq, p]], kibuf_ref.at[sq, p], sem)

    for sq in range(nseq):
        lax.fori_loop(0, npages, lambda p, _, sq=sq: (page_copy(sq, p).start(), 0)[1], 0)

    knew_ref[...] = jnp.zeros(knew_ref.shape, F32)
    wfull = []
    for sq in range(nseq):
        knew_ref[sq, 0:tq, :] = kin_ref[sq]
        qi = qi_ref[sq].astype(F32)
        for h in range(IDX_HEADS):
            qis_ref[sq, h * tq:(h + 1) * tq, :] = qi[:, h * IDX_DIM:(h + 1) * IDX_DIM]
        wi = wi_ref[sq]
        wfull.append([jnp.broadcast_to(wi[:, h:h + 1], (tq, PAGE)) for h in range(IDX_HEADS)])
    qpos_seq = past + lax.broadcasted_iota(I32, (tq, 1), 0)
    qpos = jnp.concatenate([qpos_seq] * nseq, axis=0)
    kpos = (lax.broadcasted_iota(I32, (nkc, 1, PAGE), 0) * PAGE + lax.broadcasted_iota(I32, (nkc, 1, PAGE), 2))
    valid_new = kpos[npages] <= qpos_seq

    for sq in range(nseq):
        lax.fori_loop(0, npages, lambda p, _, sq=sq: (page_copy(sq, p).wait(), 0)[1], 0)

    qis = [qis_ref[sq].astype(BF16) for sq in range(nseq)]
    rows = [slice(sq * tq, (sq + 1) * tq) for sq in range(nseq)]

    def score(cc, carry):
        carry = list(carry)
        for sq in range(nseq):
            mxp, mnp = carry[sq]
            for j in range(unroll):
                c = cc * unroll + j
                acc = _indexer_scores(_dot(qis[sq], kibuf_ref[sq, c].astype(BF16)), wfull[sq], tq)
                bias_ref[0, c, rows[sq], :] = acc
                mxp = jnp.maximum(mxp, acc)
                mnp = jnp.minimum(mnp, acc)
            carry[sq] = (mxp, mnp)
        return tuple(carry)

    init = (jnp.full((tq, PAGE), NEG_INF, F32), jnp.full((tq, PAGE), -NEG_INF, F32))
    parts = lax.fori_loop(0, npages // unroll, score, (init,) * nseq)
    mxs, mns = [], []
    for sq in range(nseq):
        acc = _indexer_scores(_dot_nt(qis[sq], knew_ref[sq].astype(BF16)), wfull[sq], tq)
        bias_ref[0, npages, rows[sq], :] = jnp.where(valid_new, acc, NEG_INF)
        mxs.append(jnp.max(jnp.maximum(parts[sq][0], jnp.where(valid_new, acc, NEG_INF)), axis=-1, keepdims=True))
        mns.append(jnp.min(jnp.minimum(parts[sq][1], jnp.where(valid_new, acc, -NEG_INF)), axis=-1, keepdims=True))
    mx = jnp.concatenate(mxs, axis=0)
    mn = jnp.concatenate(mns, axis=0)

    def count(pred):
        ind = jnp.where(pred(bias_ref[0], kpos), 1.0, 0.0)
        return jnp.sum(jnp.sum(ind, axis=0), axis=-1, keepdims=True)

    def rewrite(fn):
        bias_ref[0] = fn(bias_ref[0], kpos)

    _select_topk(count, rewrite, top, qpos + 1, mx, mn, nkc * PAGE - 1, _index_steps(nkc * PAGE))


def _index_sample(page_table, qi, wi, ki_new, cki_t, nseq):
    b, tq, _ = qi.shape
    npages = page_table.shape[1]
    top = min(TOPK_MAX, (npages * PAGE + tq) // 4)
    nkc = npages + 1
    unroll = 8 if npages % 8 == 0 else 1
    assert tq <= PAGE and b % nseq == 0 and tq % SUBLANES == 0
    blk = lambda n: pl.BlockSpec((nseq, tq, n), lambda bi, pt: (bi, 0, 0))
    grid_spec = pltpu.PrefetchScalarGridSpec(
        num_scalar_prefetch=1,
        grid=(b // nseq,),
        in_specs=[blk(IDX_WIDTH), blk(IDX_HEADS), blk(IDX_DIM), pl.BlockSpec(memory_space=pl.ANY)],
        out_specs=pl.BlockSpec((1, nkc, nseq * tq, PAGE), lambda bi, pt: (bi, 0, 0, 0)),
        scratch_shapes=[pltpu.VMEM((nseq, npages, IDX_DIM, PAGE), F32),
                        pltpu.VMEM((nseq, IDX_HEADS * tq, IDX_DIM), F32),
                        pltpu.VMEM((nseq, PAGE, IDX_DIM), F32),
                        pltpu.SemaphoreType.DMA(())],
    )
    return pl.pallas_call(
        functools.partial(_index_sample_kernel, nseq=nseq, tq=tq, npages=npages, top=top, unroll=unroll),
        grid_spec=grid_spec,
        out_shape=jax.ShapeDtypeStruct((b // nseq, nkc, nseq * tq, PAGE), F32),
        compiler_params=_cparams(("arbitrary",)),
        name="index_sample",
    )(page_table, qi, wi, ki_new, cki_t)


def _attn_sample_kernel(pt_ref, q_ref, kn_ref, vn_ref, bias_ref, ck_hbm, cv_hbm, o_ref,
                        kbuf_ref, vbuf_ref, qs_ref, knew_ref, vnew_ref, m_ref, l_ref, acc_ref, s_ref, sems,
                        *, tq, gp, npages, nb):
    b = pl.program_id(0)
    g = pl.program_id(1)
    ng = npages // gp
    step = b * ng + g
    slot = step % KV_SLOTS

    def group_copies(bb, gg, sl):
        out = []
        for j in range(gp):
            page = pt_ref[bb, gg * gp + j]
            out.append(pltpu.make_async_copy(ck_hbm.at[page], kbuf_ref.at[sl, j], sems.at[0, sl]))
            out.append(pltpu.make_async_copy(cv_hbm.at[page], vbuf_ref.at[sl, j], sems.at[1, sl]))
        return out

    def start_step(st):
        for cp in group_copies(st // ng, st % ng, st % KV_SLOTS):
            cp.start()

    ahead = KV_SLOTS - 1
    assert nb * ng >= ahead

    @pl.when(step == 0)
    def _():
        for st in range(ahead):
            start_step(st)

    @pl.when(step + ahead < nb * ng)
    def _():
        start_step(step + ahead)

    lane = lax.broadcasted_iota(I32, (tq, LANES), 1)

    @pl.when(g == 0)
    def _():
        q = q_ref[0].astype(F32)
        for hp in range(NPAIR):
            qs_ref[hp] = _split_pair(q[:, hp * LANES:(hp + 1) * LANES], lane)
        m_ref[...] = jnp.full(m_ref.shape, M_INIT, F32)
        l_ref[...] = jnp.zeros(l_ref.shape, F32)
        acc_ref[...] = jnp.zeros(acc_ref.shape, F32)

    for cp in group_copies(b, g, slot):
        cp.wait()

    def bias2(c):
        bias = bias_ref[0, c]
        return jnp.concatenate([bias, bias], axis=0)

    def stage_scores(hp, s_list):
        for j, s in enumerate(s_list):
            s_ref[hp, j] = s
        return jnp.max(_tree(jnp.maximum, s_list), axis=-1, keepdims=True)

    def apply_update(hp, n, cmax, pv_of):
        m = m_ref[hp]
        m_new = jnp.maximum(m, cmax)
        alpha = jnp.exp(m - m_new)
        ps = [jnp.exp(s_ref[hp, j] - m_new) for j in range(n)]
        l_ref[hp] = alpha * l_ref[hp] + jnp.sum(_tree(lambda a, b: a + b, ps), axis=-1, keepdims=True)
        acc_ref[hp] = alpha * acc_ref[hp] + _tree(lambda a, b: a + b,
                                                  [pv_of(j, p.astype(BF16)) for j, p in enumerate(ps)])
        m_ref[hp] = m_new

    pairs = [slice(hp * LANES, (hp + 1) * LANES) for hp in range(NPAIR)]
    biases = [bias2(g * gp + j) for j in range(gp)]
    cmax = [stage_scores(hp, [_dot(qs_ref[hp].astype(BF16), kbuf_ref[slot, j, pairs[hp], :].astype(BF16)) + biases[j]
                              for j in range(gp)]) for hp in range(NPAIR)]
    for hp in range(NPAIR):
        apply_update(hp, gp, cmax[hp],
                     lambda j, p, hp=hp: _dot_nt(p, vbuf_ref[slot, j, pairs[hp], :].astype(BF16)))

    @pl.when(g == ng - 1)
    def _():
        knew_ref[...] = jnp.zeros(knew_ref.shape, F32)
        vnew_ref[...] = jnp.zeros(vnew_ref.shape, F32)
        knew_ref[0:tq, :] = kn_ref[0]
        vnew_ref[0:tq, :] = vn_ref[0]
        bnew = bias2(npages)
        for hp in range(NPAIR):
            s = _dot_nt(qs_ref[hp].astype(BF16), knew_ref[:, pairs[hp]].astype(BF16)) + bnew
            apply_update(hp, 1, stage_scores(hp, [s]),
                         lambda j, p, hp=hp: _dot(p, vnew_ref[:, pairs[hp]].astype(BF16)))
            o_ref[0, :, pairs[hp]] = _merge_pair(acc_ref[hp] / l_ref[hp], lane, tq)


def _attn_sample(page_table, q, k_new, v_new, bias, ck_t, cv_t, gp):
    b, tq, _ = q.shape
    npages = page_table.shape[1]
    nseq = bias.shape[2] // tq
    assert npages % gp == 0 and bias.shape[1] == npages + 1 and tq <= PAGE
    blk = lambda n: pl.BlockSpec((1, tq, n), lambda bi, gi, pt: (bi, 0, 0))
    grid_spec = pltpu.PrefetchScalarGridSpec(
        num_scalar_prefetch=1,
        grid=(b, npages // gp),
        in_specs=[blk(ATT_WIDTH), blk(ATT_WIDTH), blk(ATT_WIDTH),
                  pl.BlockSpec((1, npages + 1, tq, PAGE), lambda bi, gi, pt: (bi // nseq, 0, bi % nseq, 0)),
                  pl.BlockSpec(memory_space=pl.ANY), pl.BlockSpec(memory_space=pl.ANY)],
        out_specs=blk(ATT_WIDTH),
        scratch_shapes=[pltpu.VMEM((KV_SLOTS, gp, ATT_WIDTH, PAGE), F32),
                        pltpu.VMEM((KV_SLOTS, gp, ATT_WIDTH, PAGE), F32),
                        pltpu.VMEM((NPAIR, 2 * tq, LANES), F32),
                        pltpu.VMEM((PAGE, ATT_WIDTH), F32),
                        pltpu.VMEM((PAGE, ATT_WIDTH), F32),
                        pltpu.VMEM((NPAIR, 2 * tq, 1), F32),
                        pltpu.VMEM((NPAIR, 2 * tq, 1), F32),
                        pltpu.VMEM((NPAIR, 2 * tq, LANES), F32),
                        pltpu.VMEM((NPAIR, gp, 2 * tq, PAGE), F32),
                        pltpu.SemaphoreType.DMA((2, KV_SLOTS))],
    )
    return pl.pallas_call(
        functools.partial(_attn_sample_kernel, tq=tq, gp=gp, npages=npages, nb=b),
        grid_spec=grid_spec,
        out_shape=jax.ShapeDtypeStruct((b, tq, ATT_WIDTH), F32),
        compiler_params=_cparams(("arbitrary", "arbitrary")),
        name="attn_sample",
    )(page_table, q, k_new, v_new, bias, ck_t, cv_t)


def _cumsum_rows(x):
    n = x.shape[0]
    row = lax.broadcasted_iota(I32, x.shape, 0)
    sh = 1
    while sh < n:
        x = x + jnp.where(row >= sh, pltpu.roll(x, sh, 0), 0.0)
        sh *= 2
    return x


def _hgrn_kernel(hq_ref, lf_ref, hk_ref, hv_ref, hg_ref, s0_ref, ng_ref, o_ref, sout_ref,
                 st_ref, cum_ref, *, tb, c, sb):
    t = pl.program_id(2)
    nt = pl.num_programs(2)

    @pl.when(t == 0)
    def _():
        st_ref[...] = s0_ref[0, 0].T

    row8 = lax.broadcasted_iota(I32, (SUBLANES, 1), 0)
    row_c = lax.broadcasted_iota(I32, (c, 1), 0)
    nsb = c // sb

    for j in range(tb // c):
        rows = slice(j * c, (j + 1) * c)
        lf = lf_ref[0, rows, :]
        q = hq_ref[0, rows, :]
        k = hk_ref[0, rows, :]
        v = hv_ref[0, rows, :]
        cum = _cumsum_rows(lf) * LOG2E
        cum_ref[...] = cum
        last = cum[c - 1:c, :]
        st = st_ref[...]
        o = _dot_nt((q * jnp.exp2(cum)).astype(BF16), st.astype(BF16))

        vb = v.astype(BF16)
        o_parts = []
        a_rows = []
        for i in range(nsb):
            base = i * sb
            groups = [slice(base + r, base + r + SUBLANES) for r in range(0, sb, SUBLANES)]
            od = [jnp.zeros((SUBLANES, HG_DV), F32) for _ in groups]
            for s in range(sb):
                sr = base + s
                row = slice(j * c + sr, j * c + sr + 1)
                cum_s, k_s, v_s = cum_ref[sr:sr + 1, :], hk_ref[0, row, :], hv_ref[0, row, :]
                for gi, rr in enumerate(groups):
                    first = gi * SUBLANES
                    if first + SUBLANES <= s:
                        continue
                    ex = cum[rr] - cum_s
                    if first <= s:
                        ex = jnp.where(row8 >= s - first, ex, NEG_INF)
                    a_col = jnp.sum(jnp.exp2(ex) * (q[rr] * k_s), axis=-1, keepdims=True)
                    od[gi] = od[gi] + a_col * v_s
            o_parts.extend(od)
            if i > 0:
                c0 = cum[base:base + 1, :] - lf[base:base + 1, :] * LOG2E
                qs = (q[base:base + sb] * jnp.exp2(cum[base:base + sb] - c0)).astype(BF16)
                ks = (k * jnp.exp2(jnp.where(row_c < base, c0 - cum, NEG_INF))).astype(BF16)
                a_rows.append(_dot_nt(qs, ks))
            elif nsb > 1:
                a_rows.append(jnp.zeros((sb, c), F32))
        o = o + jnp.concatenate(o_parts, axis=0)
        if nsb > 1:
            o = o + _dot(jnp.concatenate(a_rows, axis=0).astype(BF16), vb)

        kd = (k * jnp.exp2(last - cum)).astype(BF16)
        st_ref[...] = st * jnp.exp2(last) + _dot_tn(vb, kd)

        ms = jnp.mean(o * o, axis=-1, keepdims=True)
        hg = hg_ref[0, rows, :]
        o_ref[0, rows, :] = (((o * lax.rsqrt(ms + RMS_EPS)) * ng_ref[...]) * (hg * _sigmoid(hg))).astype(BF16)

    @pl.when(t == nt - 1)
    def _():
        sout_ref[0, 0] = st_ref[...].T


def _hgrn(hq, lf, hk, hv, hg, s0, norm_g, tb, c):
    b, t, _ = hq.shape
    sb = 16
    assert t % tb == 0 and tb % c == 0 and c % sb == 0
    blk = pl.BlockSpec((1, tb, HG_DK), lambda bi, h, ti: (bi, ti, h))
    sblk = pl.BlockSpec((1, 1, HG_DK, HG_DV), lambda bi, h, ti: (bi, h, 0, 0))
    return pl.pallas_call(
        functools.partial(_hgrn_kernel, tb=tb, c=c, sb=sb),
        grid=(b, HG_HEADS, t // tb),
        in_specs=[blk, blk, blk, blk, blk, sblk, pl.BlockSpec((1, HG_DV), lambda bi, h, ti: (0, 0))],
        out_specs=[blk, sblk],
        out_shape=[jax.ShapeDtypeStruct((b, t, HG_WIDTH), BF16),
                   jax.ShapeDtypeStruct((b, HG_HEADS, HG_DK, HG_DV), F32)],
        scratch_shapes=[pltpu.VMEM((HG_DV, HG_DK), F32), pltpu.VMEM((c, HG_DK), F32)],
        compiler_params=_cparams(("arbitrary", "arbitrary", "arbitrary")),
        name="hgrn",
    )(hq, lf, hk, hv, hg, s0, norm_g)


def _merge_kernel(x_ref, att_ref, ga_ref, hb_ref, ma_ref, mb_ref, p_ref,
                  wa_ref, wb_ref, wo_ref, wpg_ref, wpp_ref, gp_ref, o_ref):
    ga = ga_ref[...]
    ya = _dot((att_ref[...] * (ga * _sigmoid(ga))).astype(BF16), wa_ref[...])
    yb = _dot(hb_ref[...], wb_ref[...])
    mixed = _sigmoid(ma_ref[...]) * ya + _sigmoid(mb_ref[...]) * yb
    u = _dot(mixed.astype(BF16), wo_ref[...])
    ms = jnp.mean(u * u, axis=-1, keepdims=True)
    h1 = x_ref[...] + (u * lax.rsqrt(ms + RMS_EPS)) * gp_ref[...]
    gate = _sigmoid(_dot(h1.astype(BF16), wpg_ref[...]))
    o_ref[...] = h1 + gate * _dot(p_ref[...].astype(BF16), wpp_ref[...])


def _merge(x2, att, ga, hb, ma, mb, p2, w_a, w_b, w_o, w_pg, w_pp, g_post, tm):
    m, d = x2.shape
    assert m % tm == 0
    row = lambda a: pl.BlockSpec((tm, a.shape[1]), lambda i: (i, 0))
    full = lambda a: pl.BlockSpec(a.shape, lambda i: (0, 0))
    acts = (x2, att, ga, hb, ma, mb, p2)
    wts = (w_a, w_b, w_o, w_pg, w_pp, g_post)
    return pl.pallas_call(
        _merge_kernel,
        grid=(m // tm,),
        in_specs=[row(a) for a in acts] + [full(w) for w in wts],
        out_specs=pl.BlockSpec((tm, d), lambda i: (i, 0)),
        out_shape=jax.ShapeDtypeStruct((m, d), F32),
        compiler_params=_cparams(("arbitrary",)),
        name="merge",
    )(*acts, *wts)


def _pick_rows(m, pref):
    return pref if m % pref == 0 else m


def kernel(x_prompt, x_sample, p_prompt, p_sample, cache_k, cache_v, cache_kidx, state_hgrn, page_table,
           g_pre, g_post, w_in, hgrn_lb_logits, hgrn_norm_g, w_branch_a, w_branch_b, w_out,
           w_ple_gate, w_ple_proj):
    depth = w_in.shape[0]
    bp, tp, d = x_prompt.shape
    bs, ts, _ = x_sample.shape
    nphys = cache_k.shape[1]
    hp, hs = x_prompt, x_sample
    outs = [[] for _ in range(8)]
    w = ATT_WIDTH
    a_end = 4 * w + IDX_WIDTH
    b_end = a_end + IDX_DIM + IDX_HEADS

    for i in range(depth):
        wi_ = w_in[i].astype(BF16)
        wa, wb, wc = wi_[:, :a_end], wi_[:, a_end:b_end], wi_[:, b_end:]
        wt = jnp.concatenate([wi_[:, 0:3 * w], wi_[:, 4 * w:b_end]], axis=1).T
        wtok = jnp.concatenate([wi_[:, w:2 * w], wi_[:, 3 * w:4 * w], wc], axis=1)
        g_pre_i = g_pre[i].reshape(1, d)
        wts = (w_branch_a[i].astype(BF16), w_branch_b[i].astype(BF16), w_out[i].astype(BF16),
               w_ple_gate[i].astype(BF16), w_ple_proj[i].astype(BF16), g_post[i].reshape(1, d))
        norm_g = hgrn_norm_g[i].reshape(1, HG_DV)

        mp = bp * tp
        kc = _pick_rows(tp, 256)
        (qt, kt, ktok, vt, vtc, ga, qit, kit, kitok, wit, hq, lf, hk, hv, hg, ma, mb) = _inproj_prompt(
            hp.reshape(mp, d), g_pre_i, hgrn_lb_logits, wt, wtok, wb, i, kc, bp)
        r3 = lambda a, b_=bp, t_=tp: a.reshape(b_, t_, a.shape[-1])
        att = _attn_prompt(qit, wit, kitok, qt, ktok, vtc, qb=_pick_rows(tp, 256))
        s0 = jnp.zeros((bp, HG_HEADS, HG_DK, HG_DV), F32)
        hb, s_p = _hgrn(r3(hq), r3(lf), r3(hk), r3(hv), r3(hg), s0, norm_g, tb=_pick_rows(tp, 256), c=64)
        y = _merge(hp.reshape(mp, d), att.reshape(mp, w), ga, hb.reshape(mp, HG_WIDTH), ma, mb,
                   p_prompt[i].reshape(mp, -1), *wts, _pick_rows(mp, 256))
        hp = y.reshape(bp, tp, d)
        outs[0].append(kt.reshape(bp, N_HEADS, HEAD_DIM, tp).transpose(0, 3, 1, 2))
        outs[1].append(vt.reshape(bp, N_HEADS, HEAD_DIM, tp).transpose(0, 3, 1, 2))
        outs[2].append(kit.transpose(0, 2, 1))
        outs[3].append(s_p.astype(state_hgrn.dtype))

        ms = bs * ts
        (q, k, v, ga, qi, ki, wi, hq, lf, hk, hv, hg, ma, mb) = _inproj_sample(
            hs.reshape(ms, d), g_pre_i, hgrn_lb_logits, wa, wb, wc, i, _pick_rows(ms, 256))
        r3 = lambda a, b_=bs, t_=ts: a.reshape(b_, t_, a.shape[-1])
        cki_t = cache_kidx[i].transpose(0, 2, 1)
        ck_t = cache_k[i].transpose(0, 2, 3, 1).reshape(nphys, w, PAGE)
        cv_t = cache_v[i].transpose(0, 2, 3, 1).reshape(nphys, w, PAGE)
        bias = _index_sample(page_table, r3(qi), r3(wi), r3(ki), cki_t, nseq=4 if bs % 4 == 0 else 1)
        att = _attn_sample(page_table, r3(q), r3(k), r3(v), bias, ck_t, cv_t, gp=8)
        tpad = -ts % 16
        pad = lambda a: jnp.pad(r3(a), ((0, 0), (0, tpad), (0, 0)))
        hb, s_s = _hgrn(pad(hq), pad(lf), pad(hk), pad(hv), pad(hg), state_hgrn[i], norm_g,
                        tb=ts + tpad, c=ts + tpad)
        y = _merge(hs.reshape(ms, d), att.reshape(ms, w), ga, hb[:, :ts].reshape(ms, HG_WIDTH), ma, mb,
                   p_sample[i].reshape(ms, -1), *wts, _pick_rows(ms, 256))
        hs = y.reshape(bs, ts, d)
        outs[4].append(k.reshape(bs, ts, N_HEADS, HEAD_DIM))
        outs[5].append(v.reshape(bs, ts, N_HEADS, HEAD_DIM))
        outs[6].append(ki.reshape(bs, ts, IDX_DIM))
        outs[7].append(s_s.astype(state_hgrn.dtype))

    st = [jnp.stack(o) for o in outs]
    return (hp, hs, st[0], st[1], st[2], st[3], st[4], st[5], st[6], st[7])
```

```python
import functools

import jax
import jax.numpy as jnp
from jax import lax
from jax.experimental import pallas as pl
from jax.experimental.pallas import tpu as pltpu

F32 = jnp.float32
BF16 = jnp.bfloat16
I32 = jnp.int32

N_HEADS = 8
HEAD_DIM = 64
ATT_WIDTH = N_HEADS * HEAD_DIM
IDX_HEADS = 8
IDX_DIM = 64
IDX_WIDTH = IDX_HEADS * IDX_DIM
TOPK_MAX = 256
HG_HEADS = 4
HG_DK = 128
HG_DV = 128
HG_WIDTH = HG_HEADS * HG_DV
PAGE = 128
RMS_EPS = 1e-6

LANES = 128
SUBLANES = 8
BF16_ROWS = 16
VMEM_LIMIT = 56 * 1024 * 1024
NEG_INF = float("-inf")
M_INIT = -1e30
INT_MAG = 0x7FFFFFFF
NPAIR = N_HEADS // 2
LOG2E = 1.4426950408889634
FLOAT_BISECT_STEPS = 12
UNTESTED_PASSES = 9
PASSES_PER_TEST = 3
KV_SLOTS = 3


def _cparams(sem):
    return pltpu.CompilerParams(dimension_semantics=sem, vmem_limit_bytes=VMEM_LIMIT)


def _sigmoid(x):
    return 1.0 / (1.0 + jnp.exp(-x))


def _dot(a, b):
    return jnp.dot(a, b, preferred_element_type=F32)


def _dot_nt(a, b):
    return lax.dot_general(a, b, (((1,), (1,)), ((), ())), preferred_element_type=F32)


def _dot_tn(a, b):
    return lax.dot_general(a, b, (((0,), (0,)), ((), ())), preferred_element_type=F32)


def _tree(op, xs):
    xs = list(xs)
    while len(xs) > 1:
        xs = [op(xs[i], xs[i + 1]) if i + 1 < len(xs) else xs[i] for i in range(0, len(xs), 2)]
    return xs[0]


def _fold_rows(x, op):
    return _tree(op, [x[r:r + SUBLANES] for r in range(0, x.shape[0], SUBLANES)])


def _two_stage(n, stage_a, stage_b, carry):
    def two(j, st):
        a0, carry = st
        a1 = stage_a(2 * j + 1, 1)
        carry = stage_b(2 * j, 0, a0, carry)
        a0 = stage_a(2 * j + 2, 0)
        carry = stage_b(2 * j + 1, 1, a1, carry)
        return a0, carry

    npairs = (n - 1) // 2
    a0, carry = lax.fori_loop(0, npairs, two, (stage_a(0, 0), carry))
    last = 2 * npairs

    def tail2():
        a1 = stage_a(last + 1, 1)
        return stage_b(last + 1, 1, a1, stage_b(last, 0, a0, carry))

    return lax.cond(last + 1 < n, tail2, lambda: stage_b(last, 0, a0, carry))


def _chunk_loop2(n, body, carry):
    carry = lax.fori_loop(0, n // 2, lambda j, cr: body(2 * j + 1, body(2 * j, cr)), carry)
    return lax.cond(n % 2 == 1, lambda: body(n - 1, carry), lambda: carry)


def _normed(x_ref, g_ref):
    x = x_ref[...]
    var = jnp.mean(x * x, axis=-1, keepdims=True)
    return ((x * lax.rsqrt(var + RMS_EPS)) * g_ref[...]).astype(BF16)


def _hgrn_and_merge_gates(xn, w_ref, off, lbl_ref, layer, hq_ref, lf_ref, hk_ref, hv_ref, hg_ref, ma_ref, mb_ref):
    def proj(lo, hi):
        return _dot(xn, w_ref[:, off + lo:off + hi])

    lg = lbl_ref[...]
    e = jnp.exp(lg - jnp.max(lg, axis=0, keepdims=True))
    sm = e / jnp.sum(e, axis=0, keepdims=True)
    lb = jnp.sum(sm[:layer + 1], axis=0, keepdims=True)
    one_m = 1.0 - lb

    hw = HG_WIDTH
    hq = proj(0, hw)
    hq_ref[...] = hq * _sigmoid(hq)
    zf = proj(hw, 2 * hw)
    lf_ref[...] = jnp.log(lb + one_m * _sigmoid(zf))
    hk_ref[...] = one_m * _sigmoid(-zf)
    hv_ref[...] = proj(2 * hw, 3 * hw)
    hg_ref[...] = proj(3 * hw, 4 * hw)
    d = ma_ref.shape[-1]
    ma_ref[...] = proj(4 * hw, 4 * hw + d)
    mb_ref[...] = proj(4 * hw + d, 4 * hw + 2 * d)


def _inproj_sample_kernel(x_ref, g_ref, lbl_ref, wa_ref, wb_ref, wc_ref,
                          q_ref, k_ref, v_ref, ga_ref, qi_ref, ki_ref, wi_ref, *tail, layer):
    xn = _normed(x_ref, g_ref)
    w = ATT_WIDTH
    q_ref[...] = (_dot(xn, wa_ref[:, 0:w]) * (HEAD_DIM ** -0.5)).astype(BF16)
    k_ref[...] = _dot(xn, wa_ref[:, w:2 * w])
    v_ref[...] = _dot(xn, wa_ref[:, 2 * w:3 * w])
    ga_ref[...] = _dot(xn, wa_ref[:, 3 * w:4 * w])
    qi_ref[...] = (_dot(xn, wa_ref[:, 4 * w:4 * w + IDX_WIDTH]) * (IDX_DIM ** -0.5)).astype(BF16)
    kiwi = _dot(xn, wb_ref[...])
    ki_ref[...] = kiwi[:, :IDX_DIM]
    wi_ref[...] = kiwi[:, IDX_DIM:IDX_DIM + IDX_HEADS] * (IDX_HEADS ** -0.5)
    _hgrn_and_merge_gates(xn, wc_ref, 0, lbl_ref, layer, *tail)


def _inproj_prompt_kernel(x_ref, g_ref, lbl_ref, wt_ref, wtok_ref, wb_ref,
                          qt_ref, kt_ref, ktok_ref, vt_ref, vtc_ref, ga_ref, qit_ref, kit_ref, kitok_ref, wit_ref,
                          *tail, layer):
    xn = _normed(x_ref, g_ref)
    w = ATT_WIDTH

    def tproj(lo, hi):
        return _dot_nt(wt_ref[lo:hi, :], xn)

    qt_ref[0] = (tproj(0, w) * (HEAD_DIM ** -0.5 * LOG2E)).astype(BF16)
    kt_ref[0] = tproj(w, 2 * w)
    vt = tproj(2 * w, 3 * w)
    vt_ref[0] = vt
    vtc_ref[0, 0] = vt.astype(BF16)
    o = 3 * w
    qit_ref[0] = (tproj(o, o + IDX_WIDTH) * (IDX_DIM ** -0.5)).astype(BF16)
    o += IDX_WIDTH
    kit_ref[0] = tproj(o, o + IDX_DIM)
    wit_ref[0] = tproj(o + IDX_DIM, o + IDX_DIM + IDX_HEADS) * (IDX_HEADS ** -0.5)

    ktok_ref[0, 0] = _dot(xn, wtok_ref[:, 0:w]).astype(BF16)
    kitok_ref[0, 0] = _dot(xn, wb_ref[:, 0:IDX_DIM]).astype(BF16)
    ga_ref[...] = _dot(xn, wtok_ref[:, w:2 * w])
    _hgrn_and_merge_gates(xn, wtok_ref, 2 * w, lbl_ref, layer, *tail)


def _inproj_sample(x2, g_pre, lb_logits, wa, wb, wc, layer, tm):
    m, d = x2.shape
    assert m % tm == 0
    row = lambda n, dt: (pl.BlockSpec((tm, n), lambda i: (i, 0)), jax.ShapeDtypeStruct((m, n), dt))
    full = lambda a: pl.BlockSpec(a.shape, lambda i: (0,) * a.ndim)
    outs = [row(ATT_WIDTH, BF16), row(ATT_WIDTH, F32), row(ATT_WIDTH, F32), row(ATT_WIDTH, F32),
            row(IDX_WIDTH, BF16), row(IDX_DIM, F32), row(IDX_HEADS, F32)] + [row(HG_WIDTH, F32)] * 5 + [row(d, F32)] * 2
    ins = [x2, g_pre, lb_logits, wa, wb, wc]
    return pl.pallas_call(
        functools.partial(_inproj_sample_kernel, layer=layer),
        grid=(m // tm,),
        in_specs=[pl.BlockSpec((tm, d), lambda i: (i, 0))] + [full(a) for a in ins[1:]],
        out_specs=[o[0] for o in outs],
        out_shape=[o[1] for o in outs],
        compiler_params=_cparams(("arbitrary",)),
        name="inproj_sample",
    )(*ins)


def _inproj_prompt(x2, g_pre, lb_logits, wt, wtok, wb, layer, tm, batch):
    m, d = x2.shape
    t = m // batch
    assert m % tm == 0 and t % tm == 0
    nt = t // tm
    row = lambda n, dt: (pl.BlockSpec((tm, n), lambda i: (i, 0)), jax.ShapeDtypeStruct((m, n), dt))
    full = lambda a: pl.BlockSpec(a.shape, lambda i: (0,) * a.ndim)
    fmaj = lambda n, dt: (pl.BlockSpec((1, n, tm), lambda i: (i // nt, 0, i % nt)),
                          jax.ShapeDtypeStruct((batch, n, t), dt))
    fchunk = lambda n: (pl.BlockSpec((1, 1, n, tm), lambda i: (i // nt, i % nt, 0, 0)),
                        jax.ShapeDtypeStruct((batch, nt, n, tm), BF16))
    tchunk = lambda n: (pl.BlockSpec((1, 1, tm, n), lambda i: (i // nt, i % nt, 0, 0)),
                        jax.ShapeDtypeStruct((batch, nt, tm, n), BF16))
    outs = [fmaj(ATT_WIDTH, BF16), fmaj(ATT_WIDTH, F32), tchunk(ATT_WIDTH), fmaj(ATT_WIDTH, F32), fchunk(ATT_WIDTH),
            row(ATT_WIDTH, F32), fmaj(IDX_WIDTH, BF16), fmaj(IDX_DIM, F32), tchunk(IDX_DIM), fmaj(IDX_HEADS, F32)]
    outs += [row(HG_WIDTH, F32)] * 5 + [row(d, F32)] * 2
    ins = [x2, g_pre, lb_logits, wt, wtok, wb]
    return pl.pallas_call(
        functools.partial(_inproj_prompt_kernel, layer=layer),
        grid=(m // tm,),
        in_specs=[pl.BlockSpec((tm, d), lambda i: (i, 0))] + [full(a) for a in ins[1:]],
        out_specs=[o[0] for o in outs],
        out_shape=[o[1] for o in outs],
        compiler_params=_cparams(("arbitrary",)),
        name="inproj_prompt",
    )(*ins)


def _f2key(x):
    b = lax.bitcast_convert_type(x, I32)
    return b ^ ((b >> 31) & INT_MAG)


def _key2f(k):
    return lax.bitcast_convert_type(k ^ ((k >> 31) & INT_MAG), F32)


def _select_topk(count, rewrite, top, n_valid, mx, mn, last_pos, index_steps):
    topf = float(top)

    small = n_valid <= top
    lo0 = _f2key(mn)
    hi0 = _f2key(mx) + 1
    conv0 = jnp.logical_and(jnp.logical_not(small), lo0 + 1 >= hi0)
    done0 = jnp.logical_or(small, conv0)
    thr0 = mn
    ghi0 = jnp.zeros(mn.shape, F32)

    def zero_counts():
        return count(lambda x, kpos: x > 0.0), count(lambda x, kpos: x >= 0.0)

    n_pos, n_nonneg = lax.cond(jnp.sum(1.0 - done0.astype(F32)) > 0.0, zero_counts, lambda: (ghi0, ghi0))
    zero_tie = jnp.logical_and(jnp.logical_not(done0), jnp.logical_and(n_pos < topf, n_nonneg >= topf))
    thr0 = jnp.where(zero_tie, 0.0, thr0)
    ghi0 = jnp.where(zero_tie, n_pos, ghi0)
    tie0 = jnp.logical_or(conv0, zero_tie)
    done0 = jnp.logical_or(done0, zero_tie)
    lo0 = jnp.where(n_nonneg > topf, jnp.maximum(lo0, 0), lo0)
    below = n_nonneg < topf
    hi0 = jnp.where(below, jnp.minimum(hi0, 0), hi0)
    ghi0 = jnp.where(jnp.logical_and(below, hi0 == 0), n_nonneg, ghi0)
    n_live0 = jnp.sum(1.0 - done0.astype(F32))
    state0 = (lo0, hi0, ghi0, thr0, done0.astype(I32), tie0.astype(I32), jnp.int32(0))

    def one_pass(st):
        lo, hi, ghi, thr, done_i, tie_i, it = st
        done = done_i > 0
        lof, hif = _key2f(lo), _key2f(hi)
        midf = 0.5 * lof + 0.5 * hif
        use_f = jnp.logical_and(it < FLOAT_BISECT_STEPS, jnp.logical_and(midf > lof, midf < hif))
        mid = jnp.where(use_f, _f2key(midf), (lo & hi) + ((lo ^ hi) >> 1))
        tmid = _key2f(mid)
        g = count(lambda x, kpos: x >= tmid)
        live = jnp.logical_not(done)
        is_eq = jnp.logical_and(live, g == topf)
        up = jnp.logical_and(live, g > topf)
        down = jnp.logical_and(live, g < topf)
        lo = jnp.where(up, mid, lo)
        hi = jnp.where(down, mid, hi)
        ghi = jnp.where(down, g, ghi)
        thr = jnp.where(is_eq, tmid, thr)
        done = jnp.logical_or(done, is_eq)
        conv = jnp.logical_and(jnp.logical_not(done), lo + 1 >= hi)
        thr = jnp.where(conv, _key2f(lo), thr)
        done = jnp.logical_or(done, conv)
        tie_i = jnp.where(conv, 1, tie_i)
        return (lo, hi, ghi, thr, done.astype(I32), tie_i, it + 1)

    def passes(carry):
        st = carry[0]
        for _ in range(PASSES_PER_TEST):
            st = one_pass(st)
        return st, jnp.sum(1.0 - st[4].astype(F32))

    state = lax.cond(n_live0 > 0.0,
                     lambda: lax.fori_loop(0, UNTESTED_PASSES, lambda _, st: one_pass(st), state0),
                     lambda: state0)
    (_, _, ghi, thr, _, tie_i, _), _ = lax.while_loop(lambda carry: carry[1] > 0.0, passes, (state, n_live0))

    tie = tie_i > 0
    n_tie = jnp.sum(tie_i.astype(F32))

    @pl.when(n_tie > 0.0)
    def _():
        need = topf - ghi
        jlo0 = jnp.full(mn.shape, -1, I32)
        jhi0 = jnp.full(mn.shape, 0, I32) + last_pos

        def jbody(_, carry):
            jlo, jhi = carry
            jm = (jlo + jhi) >> 1
            cnt = count(lambda x, kpos: jnp.logical_and(x == thr, kpos <= jm))
            ok = cnt >= need
            return jnp.where(ok, jlo, jm), jnp.where(ok, jm, jhi)

        _, jcut = lax.fori_loop(0, index_steps, jbody, (jlo0, jhi0))
        rewrite(lambda x, kpos: jnp.where(
            jnp.logical_and(jnp.logical_and(tie, x == thr), kpos > jcut), NEG_INF, x))

    rewrite(lambda x, kpos: jnp.where(x >= thr, 0.0, NEG_INF))


def _index_steps(n_positions):
    return max(1, (n_positions - 1).bit_length()) + 1


def _attn_prompt_kernel(qit_ref, wit_ref, kitok_ref, qt_ref, ktok_ref, vt_ref, o_ref,
                        sc_ref, qis_ref, qs_ref, m_ref, acc_ref, s_ref, d_ref, *, qb, kc, top):
    i = pl.program_id(1)
    nkc = ((i + 1) * qb + kc - 1) // kc

    for h in range(IDX_HEADS):
        qis_ref[:, h * qb:(h + 1) * qb] = qit_ref[0, h * IDX_DIM:(h + 1) * IDX_DIM, :]
    feat = lax.broadcasted_iota(I32, (LANES, qb), 0)
    for hp in range(NPAIR):
        qp = qt_ref[0, hp * LANES:(hp + 1) * LANES, :].astype(F32)
        qs_ref[hp] = jnp.concatenate([jnp.where(feat < HEAD_DIM, qp, 0.0),
                                      jnp.where(feat >= HEAD_DIM, qp, 0.0)], axis=1).astype(BF16)

    wit = wit_ref[0]
    wrows = [wit[h:h + 1, :] for h in range(IDX_HEADS)]
    qpos = i * qb + lax.broadcasted_iota(I32, (1, qb), 1)
    krow = lax.broadcasted_iota(I32, (kc, 1), 0)

    def kpos_of(c):
        return c * kc + krow

    half = IDX_HEADS * qb // 2

    def score_a(c, slot):
        kic = kitok_ref[0, c]
        d_ref[slot, :, 0:half] = _dot(kic, qis_ref[:, 0:half])
        d_ref[slot, :, half:2 * half] = _dot(kic, qis_ref[:, half:2 * half])
        return ()

    def score_b(c, slot, _, carry):
        mxp, mnp = carry
        acc = _tree(lambda a, b: a + b,
                    [wrows[h] * jnp.maximum(d_ref[slot, :, h * qb:(h + 1) * qb], 0.0) for h in range(IDX_HEADS)])
        valid = kpos_of(c) <= qpos
        x = jnp.where(valid, acc, NEG_INF)
        sc_ref[c] = x
        mxp = jnp.maximum(mxp, _fold_rows(x, jnp.maximum))
        mnp = jnp.minimum(mnp, _fold_rows(jnp.where(valid, acc, -NEG_INF), jnp.minimum))
        return mxp, mnp

    mxp, mnp = _two_stage(nkc, score_a, score_b, (jnp.full((SUBLANES, qb), NEG_INF, F32),
                                                  jnp.full((SUBLANES, qb), -NEG_INF, F32)))
    mx = jnp.max(mxp, axis=0, keepdims=True)
    mn = jnp.min(mnp, axis=0, keepdims=True)

    def count(pred):
        def body(c, acc):
            return acc + _fold_rows(jnp.where(pred(sc_ref[c], kpos_of(c)), 1.0, 0.0), lambda a, b: a + b)
        return jnp.sum(_chunk_loop2(nkc, body, jnp.zeros((SUBLANES, qb), F32)), axis=0, keepdims=True)

    def rewrite(fn):
        def body(c, _):
            sc_ref[c] = fn(sc_ref[c], kpos_of(c))
            return 0
        lax.fori_loop(0, nkc, body, 0)

    _select_topk(count, rewrite, top, qpos + 1, mx, mn, nkc * kc - 1, _index_steps(sc_ref.shape[0] * kc))

    m_ref[...] = jnp.full(m_ref.shape, M_INIT, F32)
    acc_ref[...] = jnp.zeros(acc_ref.shape, F32)
    ones = jnp.ones((BF16_ROWS, kc), BF16)

    def att_a(c, slot):
        bias = sc_ref[c]
        bias2 = jnp.concatenate([bias, bias], axis=1)
        cmax = []
        for hp in range(NPAIR):
            s = _dot(ktok_ref[0, c, :, hp * LANES:(hp + 1) * LANES], qs_ref[hp]) + bias2
            s_ref[slot, hp] = s
            cmax.append(jnp.max(_fold_rows(s, jnp.maximum), axis=0, keepdims=True))
        return cmax

    def att_b(c, slot, cmax, carry):
        for hp in range(NPAIR):
            m = m_ref[hp]
            m_new = jnp.maximum(m, cmax[hp])
            p = jnp.exp2(s_ref[slot, hp] - m_new).astype(BF16)
            v1 = jnp.concatenate([vt_ref[0, c, hp * LANES:(hp + 1) * LANES, :], ones], axis=0)
            acc_ref[hp] = jnp.exp2(m - m_new) * acc_ref[hp] + _dot(v1, p)
            m_ref[hp] = m_new
        return carry

    _two_stage(nkc, att_a, att_b, 0)

    for hp in range(NPAIR):
        acc = acc_ref[hp]
        out = acc[0:LANES] / acc[LANES:LANES + 1]
        out = jnp.where(feat < HEAD_DIM, out[:, 0:qb], out[:, qb:2 * qb])
        o_ref[0, :, hp * LANES:(hp + 1) * LANES] = out.T


def _attn_prompt(qit, wit, kitok, qt, ktok, vtc, qb):
    b, _, t = qt.shape
    nc, kc = ktok.shape[1], ktok.shape[2]
    top = min(TOPK_MAX, t // 4)
    assert t % qb == 0 and nc * kc == t and kc % qb == 0 and qb % LANES == 0
    qblk = lambda n: pl.BlockSpec((1, n, qb), lambda bi, i: (bi, 0, i))
    whole = lambda a: pl.BlockSpec((1,) + a.shape[1:], lambda bi, i: (bi, 0, 0, 0), pipeline_mode=pl.Buffered(1))
    return pl.pallas_call(
        functools.partial(_attn_prompt_kernel, qb=qb, kc=kc, top=top),
        grid=(b, t // qb),
        in_specs=[qblk(IDX_WIDTH), qblk(IDX_HEADS), whole(kitok), qblk(ATT_WIDTH), whole(ktok), whole(vtc)],
        out_specs=pl.BlockSpec((1, qb, ATT_WIDTH), lambda bi, i: (bi, i, 0)),
        out_shape=jax.ShapeDtypeStruct((b, t, ATT_WIDTH), F32),
        scratch_shapes=[pltpu.VMEM((nc, kc, qb), F32),
                        pltpu.VMEM((IDX_DIM, IDX_HEADS * qb), BF16),
                        pltpu.VMEM((NPAIR, LANES, 2 * qb), BF16),
                        pltpu.VMEM((NPAIR, 1, 2 * qb), F32),
                        pltpu.VMEM((NPAIR, LANES + BF16_ROWS, 2 * qb), F32),
                        pltpu.VMEM((2, NPAIR, kc, 2 * qb), F32),
                        pltpu.VMEM((2, kc, IDX_HEADS * qb), F32)],
        compiler_params=_cparams(("arbitrary", "arbitrary")),
        name="attn_prompt",
    )(qit, wit, kitok, qt, ktok, vtc)


def _split_pair(qp, lane):
    return jnp.concatenate([jnp.where(lane < HEAD_DIM, qp, 0.0), jnp.where(lane >= HEAD_DIM, qp, 0.0)], axis=0)


def _merge_pair(x, lane, rows):
    return jnp.where(lane < HEAD_DIM, x[0:rows], x[rows:2 * rows])


def _indexer_scores(d, wfull, rows):
    return _tree(lambda a, b: a + b,
                 [wfull[h] * jnp.maximum(d[h * rows:(h + 1) * rows], 0.0) for h in range(IDX_HEADS)])


def _index_sample_kernel(pt_ref, qi_ref, wi_ref, kin_ref, cki_hbm, bias_ref,
                         kibuf_ref, qis_ref, knew_ref, sem, *, nseq, tq, npages, top, unroll):
    b0 = pl.program_id(0) * nseq
    past = npages * PAGE
    nkc = npages + 1

    def page_copy(sq, p):
        return pltpu.make_async_copy(cki_hbm.at[pt_ref[b0 + sq, p]], kibuf_ref.at[sq, p], sem)

    for sq in range(nseq):
        lax.fori_loop(0, npages, lambda p, _, sq=sq: (page_copy(sq, p).start(), 0)[1], 0)

    knew_ref[...] = jnp.zeros(knew_ref.shape, F32)
    wfull = []
    for sq in range(nseq):
        knew_ref[sq, 0:tq, :] = kin_ref[sq]
        qi = qi_ref[sq].astype(F32)
        for h in range(IDX_HEADS):
            qis_ref[sq, h * tq:(h + 1) * tq, :] = qi[:, h * IDX_DIM:(h + 1) * IDX_DIM]
        wi = wi_ref[sq]
        wfull.append([jnp.broadcast_to(wi[:, h:h + 1], (tq, PAGE)) for h in range(IDX_HEADS)])
    qpos_seq = past + lax.broadcasted_iota(I32, (tq, 1), 0)
    qpos = jnp.concatenate([qpos_seq] * nseq, axis=0)
    kpos = (lax.broadcasted_iota(I32, (nkc, 1, PAGE), 0) * PAGE + lax.broadcasted_iota(I32, (nkc, 1, PAGE), 2))
    valid_new = kpos[npages] <= qpos_seq

    for sq in range(nseq):
        lax.fori_loop(0, npages, lambda p, _, sq=sq: (page_copy(sq, p).wait(), 0)[1], 0)

    qis = [qis_ref[sq].astype(BF16) for sq in range(nseq)]
    rows = [slice(sq * tq, (sq + 1) * tq) for sq in range(nseq)]

    def score(cc, carry):
        carry = list(carry)
        for sq in range(nseq):
            mxp, mnp = carry[sq]
            for j in range(unroll):
                c = cc * unroll + j
                acc = _indexer_scores(_dot(qis[sq], kibuf_ref[sq, c].astype(BF16)), wfull[sq], tq)
                bias_ref[0, c, rows[sq], :] = acc
                mxp = jnp.maximum(mxp, acc)
                mnp = jnp.minimum(mnp, acc)
            carry[sq] = (mxp, mnp)
        return tuple(carry)

    init = (jnp.full((tq, PAGE), NEG_INF, F32), jnp.full((tq, PAGE), -NEG_INF, F32))
    parts = lax.fori_loop(0, npages // unroll, score, (init,) * nseq)
    mxs, mns = [], []
    for sq in range(nseq):
        acc = _indexer_scores(_dot_nt(qis[sq], knew_ref[sq].astype(BF16)), wfull[sq], tq)
        bias_ref[0, npages, rows[sq], :] = jnp.where(valid_new, acc, NEG_INF)
        mxs.append(jnp.max(jnp.maximum(parts[sq][0], jnp.where(valid_new, acc, NEG_INF)), axis=-1, keepdims=True))
        mns.append(jnp.min(jnp.minimum(parts[sq][1], jnp.where(valid_new, acc, -NEG_INF)), axis=-1, keepdims=True))
    mx = jnp.concatenate(mxs, axis=0)
    mn = jnp.concatenate(mns, axis=0)

    def count(pred):
        ind = jnp.where(pred(bias_ref[0], kpos), 1.0, 0.0)
        return jnp.sum(jnp.sum(ind, axis=0), axis=-1, keepdims=True)

    def rewrite(fn):
        bias_ref[0] = fn(bias_ref[0], kpos)

    _select_topk(count, rewrite, top, qpos + 1, mx, mn, nkc * PAGE - 1, _index_steps(nkc * PAGE))


def _index_sample(page_table, qi, wi, ki_new, cki_t, nseq):
    b, tq, _ = qi.shape
    npages = page_table.shape[1]
    top = min(TOPK_MAX, (npages * PAGE + tq) // 4)
    nkc = npages + 1
    unroll = 8 if npages % 8 == 0 else 1
    assert tq <= PAGE and b % nseq == 0 and tq % SUBLANES == 0
    blk = lambda n: pl.BlockSpec((nseq, tq, n), lambda bi, pt: (bi, 0, 0))
    grid_spec = pltpu.PrefetchScalarGridSpec(
        num_scalar_prefetch=1,
        grid=(b // nseq,),
        in_specs=[blk(IDX_WIDTH), blk(IDX_HEADS), blk(IDX_DIM), pl.BlockSpec(memory_space=pl.ANY)],
        out_specs=pl.BlockSpec((1, nkc, nseq * tq, PAGE), lambda bi, pt: (bi, 0, 0, 0)),
        scratch_shapes=[pltpu.VMEM((nseq, npages, IDX_DIM, PAGE), F32),
                        pltpu.VMEM((nseq, IDX_HEADS * tq, IDX_DIM), F32),
                        pltpu.VMEM((nseq, PAGE, IDX_DIM), F32),
                        pltpu.SemaphoreType.DMA(())],
    )
    return pl.pallas_call(
        functools.partial(_index_sample_kernel, nseq=nseq, tq=tq, npages=npages, top=top, unroll=unroll),
        grid_spec=grid_spec,
        out_shape=jax.ShapeDtypeStruct((b // nseq, nkc, nseq * tq, PAGE), F32),
        compiler_params=_cparams(("arbitrary",)),
        name="index_sample",
    )(page_table, qi, wi, ki_new, cki_t)


def _attn_sample_kernel(pt_ref, q_ref, kn_ref, vn_ref, bias_ref, ck_hbm, cv_hbm, o_ref,
                        kbuf_ref, vbuf_ref, qs_ref, knew_ref, vnew_ref, m_ref, l_ref, acc_ref, s_ref, sems,
                        *, tq, gp, npages, nb):
    b = pl.program_id(0)
    g = pl.program_id(1)
    ng = npages // gp
    step = b * ng + g
    slot = step % KV_SLOTS

    def group_copies(bb, gg, sl):
        out = []
        for j in range(gp):
            page = pt_ref[bb, gg * gp + j]
            out.append(pltpu.make_async_copy(ck_hbm.at[page], kbuf_ref.at[sl, j], sems.at[0, sl]))
            out.append(pltpu.make_async_copy(cv_hbm.at[page], vbuf_ref.at[sl, j], sems.at[1, sl]))
        return out

    def start_step(st):
        for cp in group_copies(st // ng, st % ng, st % KV_SLOTS):
            cp.start()

    ahead = KV_SLOTS - 1
    assert nb * ng >= ahead

    @pl.when(step == 0)
    def _():
        for st in range(ahead):
            start_step(st)

    @pl.when(step + ahead < nb * ng)
    def _():
        start_step(step + ahead)

    lane = lax.broadcasted_iota(I32, (tq, LANES), 1)

    @pl.when(g == 0)
    def _():
        q = q_ref[0].astype(F32)
        for hp in range(NPAIR):
            qs_ref[hp] = _split_pair(q[:, hp * LANES:(hp + 1) * LANES], lane)
        m_ref[...] = jnp.full(m_ref.shape, M_INIT, F32)
        l_ref[...] = jnp.zeros(l_ref.shape, F32)
        acc_ref[...] = jnp.zeros(acc_ref.shape, F32)

    for cp in group_copies(b, g, slot):
        cp.wait()

    def bias2(c):
        bias = bias_ref[0, c]
        return jnp.concatenate([bias, bias], axis=0)

    def stage_scores(hp, s_list):
        for j, s in enumerate(s_list):
            s_ref[hp, j] = s
        return jnp.max(_tree(jnp.maximum, s_list), axis=-1, keepdims=True)

    def apply_update(hp, n, cmax, pv_of):
        m = m_ref[hp]
        m_new = jnp.maximum(m, cmax)
        alpha = jnp.exp(m - m_new)
        ps = [jnp.exp(s_ref[hp, j] - m_new) for j in range(n)]
        l_ref[hp] = alpha * l_ref[hp] + jnp.sum(_tree(lambda a, b: a + b, ps), axis=-1, keepdims=True)
        acc_ref[hp] = alpha * acc_ref[hp] + _tree(lambda a, b: a + b,
                                                  [pv_of(j, p.astype(BF16)) for j, p in enumerate(ps)])
        m_ref[hp] = m_new

    pairs = [slice(hp * LANES, (hp + 1) * LANES) for hp in range(NPAIR)]
    biases = [bias2(g * gp + j) for j in range(gp)]
    cmax = [stage_scores(hp, [_dot(qs_ref[hp].astype(BF16), kbuf_ref[slot, j, pairs[hp], :].astype(BF16)) + biases[j]
                              for j in range(gp)]) for hp in range(NPAIR)]
    for hp in range(NPAIR):
        apply_update(hp, gp, cmax[hp],
                     lambda j, p, hp=hp: _dot_nt(p, vbuf_ref[slot, j, pairs[hp], :].astype(BF16)))

    @pl.when(g == ng - 1)
    def _():
        knew_ref[...] = jnp.zeros(knew_ref.shape, F32)
        vnew_ref[...] = jnp.zeros(vnew_ref.shape, F32)
        knew_ref[0:tq, :] = kn_ref[0]
        vnew_ref[0:tq, :] = vn_ref[0]
        bnew = bias2(npages)
        for hp in range(NPAIR):
            s = _dot_nt(qs_ref[hp].astype(BF16), knew_ref[:, pairs[hp]].astype(BF16)) + bnew
            apply_update(hp, 1, stage_scores(hp, [s]),
                         lambda j, p, hp=hp: _dot(p, vnew_ref[:, pairs[hp]].astype(BF16)))
            o_ref[0, :, pairs[hp]] = _merge_pair(acc_ref[hp] / l_ref[hp], lane, tq)


def _attn_sample(page_table, q, k_new, v_new, bias, ck_t, cv_t, gp):
    b, tq, _ = q.shape
    npages = page_table.shape[1]
    nseq = bias.shape[2] // tq
    assert npages % gp == 0 and bias.shape[1] == npages + 1 and tq <= PAGE
    blk = lambda n: pl.BlockSpec((1, tq, n), lambda bi, gi, pt: (bi, 0, 0))
    grid_spec = pltpu.PrefetchScalarGridSpec(
        num_scalar_prefetch=1,
        grid=(b, npages // gp),
        in_specs=[blk(ATT_WIDTH), blk(ATT_WIDTH), blk(ATT_WIDTH),
                  pl.BlockSpec((1, npages + 1, tq, PAGE), lambda bi, gi, pt: (bi // nseq, 0, bi % nseq, 0)),
                  pl.BlockSpec(memory_space=pl.ANY), pl.BlockSpec(memory_space=pl.ANY)],
        out_specs=blk(ATT_WIDTH),
        scratch_shapes=[pltpu.VMEM((KV_SLOTS, gp, ATT_WIDTH, PAGE), F32),
                        pltpu.VMEM((KV_SLOTS, gp, ATT_WIDTH, PAGE), F32),
                        pltpu.VMEM((NPAIR, 2 * tq, LANES), F32),
                        pltpu.VMEM((PAGE, ATT_WIDTH), F32),
                        pltpu.VMEM((PAGE, ATT_WIDTH), F32),
                        pltpu.VMEM((NPAIR, 2 * tq, 1), F32),
                        pltpu.VMEM((NPAIR, 2 * tq, 1), F32),
                        pltpu.VMEM((NPAIR, 2 * tq, LANES), F32),
                        pltpu.VMEM((NPAIR, gp, 2 * tq, PAGE), F32),
                        pltpu.SemaphoreType.DMA((2, KV_SLOTS))],
    )
    return pl.pallas_call(
        functools.partial(_attn_sample_kernel, tq=tq, gp=gp, npages=npages, nb=b),
        grid_spec=grid_spec,
        out_shape=jax.ShapeDtypeStruct((b, tq, ATT_WIDTH), F32),
        compiler_params=_cparams(("arbitrary", "arbitrary")),
        name="attn_sample",
    )(page_table, q, k_new, v_new, bias, ck_t, cv_t)


def _cumsum_rows(x):
    n = x.shape[0]
    row = lax.broadcasted_iota(I32, x.shape, 0)
    sh = 1
    while sh < n:
        x = x + jnp.where(row >= sh, pltpu.roll(x, sh, 0), 0.0)
        sh *= 2
    return x


def _hgrn_kernel(hq_ref, lf_ref, hk_ref, hv_ref, hg_ref, s0_ref, ng_ref, o_ref, sout_ref,
                 st_ref, cum_ref, *, tb, c, sb):
    t = pl.program_id(2)
    nt = pl.num_programs(2)

    @pl.when(t == 0)
    def _():
        st_ref[...] = s0_ref[0, 0].T

    row8 = lax.broadcasted_iota(I32, (SUBLANES, 1), 0)
    row_c = lax.broadcasted_iota(I32, (c, 1), 0)
    nsb = c // sb

    for j in range(tb // c):
        rows = slice(j * c, (j + 1) * c)
        lf = lf_ref[0, rows, :]
        q = hq_ref[0, rows, :]
        k = hk_ref[0, rows, :]
        v = hv_ref[0, rows, :]
        cum = _cumsum_rows(lf) * LOG2E
        cum_ref[...] = cum
        last = cum[c - 1:c, :]
        st = st_ref[...]
        o = _dot_nt((q * jnp.exp2(cum)).astype(BF16), st.astype(BF16))

        vb = v.astype(BF16)
        o_parts = []
        a_rows = []
        for i in range(nsb):
            base = i * sb
            groups = [slice(base + r, base + r + SUBLANES) for r in range(0, sb, SUBLANES)]
            od = [jnp.zeros((SUBLANES, HG_DV), F32) for _ in groups]
            for s in range(sb):
                sr = base + s
                row = slice(j * c + sr, j * c + sr + 1)
                cum_s, k_s, v_s = cum_ref[sr:sr + 1, :], hk_ref[0, row, :], hv_ref[0, row, :]
                for gi, rr in enumerate(groups):
                    first = gi * SUBLANES
                    if first + SUBLANES <= s:
                        continue
                    ex = cum[rr] - cum_s
                    if first <= s:
                        ex = jnp.where(row8 >= s - first, ex, NEG_INF)
                    a_col = jnp.sum(jnp.exp2(ex) * (q[rr] * k_s), axis=-1, keepdims=True)
                    od[gi] = od[gi] + a_col * v_s
            o_parts.extend(od)
            if i > 0:
                c0 = cum[base:base + 1, :] - lf[base:base + 1, :] * LOG2E
                qs = (q[base:base + sb] * jnp.exp2(cum[base:base + sb] - c0)).astype(BF16)
                ks = (k * jnp.exp2(jnp.where(row_c < base, c0 - cum, NEG_INF))).astype(BF16)
                a_rows.append(_dot_nt(qs, ks))
            elif nsb > 1:
                a_rows.append(jnp.zeros((sb, c), F32))
        o = o + jnp.concatenate(o_parts, axis=0)
        if nsb > 1:
            o = o + _dot(jnp.concatenate(a_rows, axis=0).astype(BF16), vb)

        kd = (k * jnp.exp2(last - cum)).astype(BF16)
        st_ref[...] = st * jnp.exp2(last) + _dot_tn(vb, kd)

        ms = jnp.mean(o * o, axis=-1, keepdims=True)
        hg = hg_ref[0, rows, :]
        o_ref[0, rows, :] = (((o * lax.rsqrt(ms + RMS_EPS)) * ng_ref[...]) * (hg * _sigmoid(hg))).astype(BF16)

    @pl.when(t == nt - 1)
    def _():
        sout_ref[0, 0] = st_ref[...].T


def _hgrn(hq, lf, hk, hv, hg, s0, norm_g, tb, c):
    b, t, _ = hq.shape
    sb = 16
    assert t % tb == 0 and tb % c == 0 and c % sb == 0
    blk = pl.BlockSpec((1, tb, HG_DK), lambda bi, h, ti: (bi, ti, h))
    sblk = pl.BlockSpec((1, 1, HG_DK, HG_DV), lambda bi, h, ti: (bi, h, 0, 0))
    return pl.pallas_call(
        functools.partial(_hgrn_kernel, tb=tb, c=c, sb=sb),
        grid=(b, HG_HEADS, t // tb),
        in_specs=[blk, blk, blk, blk, blk, sblk, pl.BlockSpec((1, HG_DV), lambda bi, h, ti: (0, 0))],
        out_specs=[blk, sblk],
        out_shape=[jax.ShapeDtypeStruct((b, t, HG_WIDTH), BF16),
                   jax.ShapeDtypeStruct((b, HG_HEADS, HG_DK, HG_DV), F32)],
        scratch_shapes=[pltpu.VMEM((HG_DV, HG_DK), F32), pltpu.VMEM((c, HG_DK), F32)],
        compiler_params=_cparams(("arbitrary", "arbitrary", "arbitrary")),
        name="hgrn",
    )(hq, lf, hk, hv, hg, s0, norm_g)


def _merge_kernel(x_ref, att_ref, ga_ref, hb_ref, ma_ref, mb_ref, p_ref,
                  wa_ref, wb_ref, wo_ref, wpg_ref, wpp_ref, gp_ref, o_ref):
    ga = ga_ref[...]
    ya = _dot((att_ref[...] * (ga * _sigmoid(ga))).astype(BF16), wa_ref[...])
    yb = _dot(hb_ref[...], wb_ref[...])
    mixed = _sigmoid(ma_ref[...]) * ya + _sigmoid(mb_ref[...]) * yb
    u = _dot(mixed.astype(BF16), wo_ref[...])
    ms = jnp.mean(u * u, axis=-1, keepdims=True)
    h1 = x_ref[...] + (u * lax.rsqrt(ms + RMS_EPS)) * gp_ref[...]
    gate = _sigmoid(_dot(h1.astype(BF16), wpg_ref[...]))
    o_ref[...] = h1 + gate * _dot(p_ref[...].astype(BF16), wpp_ref[...])


def _merge(x2, att, ga, hb, ma, mb, p2, w_a, w_b, w_o, w_pg, w_pp, g_post, tm):
    m, d = x2.shape
    assert m % tm == 0
    row = lambda a: pl.BlockSpec((tm, a.shape[1]), lambda i: (i, 0))
    full = lambda a: pl.BlockSpec(a.shape, lambda i: (0, 0))
    acts = (x2, att, ga, hb, ma, mb, p2)
    wts = (w_a, w_b, w_o, w_pg, w_pp, g_post)
    return pl.pallas_call(
        _merge_kernel,
        grid=(m // tm,),
        in_specs=[row(a) for a in acts] + [full(w) for w in wts],
        out_specs=pl.BlockSpec((tm, d), lambda i: (i, 0)),
        out_shape=jax.ShapeDtypeStruct((m, d), F32),
        compiler_params=_cparams(("arbitrary",)),
        name="merge",
    )(*acts, *wts)


def _pick_rows(m, pref):
    return pref if m % pref == 0 else m


def kernel(x_prompt, x_sample, p_prompt, p_sample, cache_k, cache_v, cache_kidx, state_hgrn, page_table,
           g_pre, g_post, w_in, hgrn_lb_logits, hgrn_norm_g, w_branch_a, w_branch_b, w_out,
           w_ple_gate, w_ple_proj):
    depth = w_in.shape[0]
    bp, tp, d = x_prompt.shape
    bs, ts, _ = x_sample.shape
    nphys = cache_k.shape[1]
    hp, hs = x_prompt, x_sample
    outs = [[] for _ in range(8)]
    w = ATT_WIDTH
    a_end = 4 * w + IDX_WIDTH
    b_end = a_end + IDX_DIM + IDX_HEADS

    for i in range(depth):
        wi_ = w_in[i].astype(BF16)
        wa, wb, wc = wi_[:, :a_end], wi_[:, a_end:b_end], wi_[:, b_end:]
        wt = jnp.concatenate([wi_[:, 0:3 * w], wi_[:, 4 * w:b_end]], axis=1).T
        wtok = jnp.concatenate([wi_[:, w:2 * w], wi_[:, 3 * w:4 * w], wc], axis=1)
        g_pre_i = g_pre[i].reshape(1, d)
        wts = (w_branch_a[i].astype(BF16), w_branch_b[i].astype(BF16), w_out[i].astype(BF16),
               w_ple_gate[i].astype(BF16), w_ple_proj[i].astype(BF16), g_post[i].reshape(1, d))
        norm_g = hgrn_norm_g[i].reshape(1, HG_DV)

        mp = bp * tp
        kc = _pick_rows(tp, 256)
        (qt, kt, ktok, vt, vtc, ga, qit, kit, kitok, wit, hq, lf, hk, hv, hg, ma, mb) = _inproj_prompt(
            hp.reshape(mp, d), g_pre_i, hgrn_lb_logits, wt, wtok, wb, i, kc, bp)
        r3 = lambda a, b_=bp, t_=tp: a.reshape(b_, t_, a.shape[-1])
        att = _attn_prompt(qit, wit, kitok, qt, ktok, vtc, qb=_pick_rows(tp, 256))
        s0 = jnp.zeros((bp, HG_HEADS, HG_DK, HG_DV), F32)
        hb, s_p = _hgrn(r3(hq), r3(lf), r3(hk), r3(hv), r3(hg), s0, norm_g, tb=_pick_rows(tp, 256), c=64)
        y = _merge(hp.reshape(mp, d), att.reshape(mp, w), ga, hb.reshape(mp, HG_WIDTH), ma, mb,
                   p_prompt[i].reshape(mp, -1), *wts, _pick_rows(mp, 256))
        hp = y.reshape(bp, tp, d)
        outs[0].append(kt.reshape(bp, N_HEADS, HEAD_DIM, tp).transpose(0, 3, 1, 2))
        outs[1].append(vt.reshape(bp, N_HEADS, HEAD_DIM, tp).transpose(0, 3, 1, 2))
        outs[2].append(kit.transpose(0, 2, 1))
        outs[3].append(s_p.astype(state_hgrn.dtype))

        ms = bs * ts
        (q, k, v, ga, qi, ki, wi, hq, lf, hk, hv, hg, ma, mb) = _inproj_sample(
            hs.reshape(ms, d), g_pre_i, hgrn_lb_logits, wa, wb, wc, i, _pick_rows(ms, 256))
        r3 = lambda a, b_=bs, t_=ts: a.reshape(b_, t_, a.shape[-1])
        cki_t = cache_kidx[i].transpose(0, 2, 1)
        ck_t = cache_k[i].transpose(0, 2, 3, 1).reshape(nphys, w, PAGE)
        cv_t = cache_v[i].transpose(0, 2, 3, 1).reshape(nphys, w, PAGE)
        bias = _index_sample(page_table, r3(qi), r3(wi), r3(ki), cki_t, nseq=next(n for n in (8, 4, 2, 1) if bs % n == 0))
        att = _attn_sample(page_table, r3(q), r3(k), r3(v), bias, ck_t, cv_t, gp=8)
        tpad = -ts % 16
        pad = lambda a: jnp.pad(r3(a), ((0, 0), (0, tpad), (0, 0)))
        hb, s_s = _hgrn(pad(hq), pad(lf), pad(hk), pad(hv), pad(hg), state_hgrn[i], norm_g,
                        tb=ts + tpad, c=ts + tpad)
        y = _merge(hs.reshape(ms, d), att.reshape(ms, w), ga, hb[:, :ts].reshape(ms, HG_WIDTH), ma, mb,
                   p_sample[i].reshape(ms, -1), *wts, _pick_rows(ms, 256))
        hs = y.reshape(bs, ts, d)
        outs[4].append(k.reshape(bs, ts, N_HEADS, HEAD_DIM))
        outs[5].append(v.reshape(bs, ts, N_HEADS, HEAD_DIM))
        outs[6].append(ki.reshape(bs, ts, IDX_DIM))
        outs[7].append(s_s.astype(state_hgrn.dtype))

    st = [jnp.stack(o) for o in outs]
    return (hp, hs, st[0], st[1], st[2], st[3], st[4], st[5], st[6], st[7])
```

```python
import functools

import jax
import jax.numpy as jnp
from jax import lax
from jax.experimental import pallas as pl
from jax.experimental.pallas import tpu as pltpu

F32 = jnp.float32
BF16 = jnp.bfloat16
I32 = jnp.int32

N_HEADS = 8
HEAD_DIM = 64
ATT_WIDTH = N_HEADS * HEAD_DIM
IDX_HEADS = 8
IDX_DIM = 64
IDX_WIDTH = IDX_HEADS * IDX_DIM
TOPK_MAX = 256
HG_HEADS = 4
HG_DK = 128
HG_DV = 128
HG_WIDTH = HG_HEADS * HG_DV
PAGE = 128
RMS_EPS = 1e-6

LANES = 128
SUBLANES = 8
BF16_ROWS = 16
VMEM_LIMIT = 56 * 1024 * 1024
NEG_INF = float("-inf")
M_INIT = -1e30
INT_MAG = 0x7FFFFFFF
NPAIR = N_HEADS // 2
LOG2E = 1.4426950408889634
FLOAT_BISECT_STEPS = 12
UNTESTED_PASSES = 9
PASSES_PER_TEST = 3
KV_SLOTS = 3


def _cparams(sem):
    return pltpu.CompilerParams(dimension_semantics=sem, vmem_limit_bytes=VMEM_LIMIT)


def _sigmoid(x):
    return 1.0 / (1.0 + jnp.exp(-x))


def _dot(a, b):
    return jnp.dot(a, b, preferred_element_type=F32)


def _dot_nt(a, b):
    return lax.dot_general(a, b, (((1,), (1,)), ((), ())), preferred_element_type=F32)


def _dot_tn(a, b):
    return lax.dot_general(a, b, (((0,), (0,)), ((), ())), preferred_element_type=F32)


def _tree(op, xs):
    xs = list(xs)
    while len(xs) > 1:
        xs = [op(xs[i], xs[i + 1]) if i + 1 < len(xs) else xs[i] for i in range(0, len(xs), 2)]
    return xs[0]


def _fold_rows(x, op):
    return _tree(op, [x[r:r + SUBLANES] for r in range(0, x.shape[0], SUBLANES)])


def _two_stage(n, stage_a, stage_b, carry):
    def two(j, st):
        a0, carry = st
        a1 = stage_a(2 * j + 1, 1)
        carry = stage_b(2 * j, 0, a0, carry)
        a0 = stage_a(2 * j + 2, 0)
        carry = stage_b(2 * j + 1, 1, a1, carry)
        return a0, carry

    npairs = (n - 1) // 2
    a0, carry = lax.fori_loop(0, npairs, two, (stage_a(0, 0), carry))
    last = 2 * npairs

    def tail2():
        a1 = stage_a(last + 1, 1)
        return stage_b(last + 1, 1, a1, stage_b(last, 0, a0, carry))

    return lax.cond(last + 1 < n, tail2, lambda: stage_b(last, 0, a0, carry))


def _chunk_loop2(n, body, carry):
    carry = lax.fori_loop(0, n // 2, lambda j, cr: body(2 * j + 1, body(2 * j, cr)), carry)
    return lax.cond(n % 2 == 1, lambda: body(n - 1, carry), lambda: carry)


def _normed(x_ref, g_ref):
    x = x_ref[...]
    var = jnp.mean(x * x, axis=-1, keepdims=True)
    return ((x * lax.rsqrt(var + RMS_EPS)) * g_ref[...]).astype(BF16)


def _hgrn_and_merge_gates(xn, w_ref, off, lbl_ref, layer, hq_ref, lf_ref, hk_ref, hv_ref, hg_ref, ma_ref, mb_ref):
    def proj(lo, hi):
        return _dot(xn, w_ref[:, off + lo:off + hi])

    lg = lbl_ref[...]
    e = jnp.exp(lg - jnp.max(lg, axis=0, keepdims=True))
    sm = e / jnp.sum(e, axis=0, keepdims=True)
    lb = jnp.sum(sm[:layer + 1], axis=0, keepdims=True)
    one_m = 1.0 - lb

    hw = HG_WIDTH
    hq = proj(0, hw)
    hq_ref[...] = hq * _sigmoid(hq)
    zf = proj(hw, 2 * hw)
    lf_ref[...] = jnp.log(lb + one_m * _sigmoid(zf))
    hk_ref[...] = one_m * _sigmoid(-zf)
    hv_ref[...] = proj(2 * hw, 3 * hw)
    hg_ref[...] = proj(3 * hw, 4 * hw)
    d = ma_ref.shape[-1]
    ma_ref[...] = proj(4 * hw, 4 * hw + d)
    mb_ref[...] = proj(4 * hw + d, 4 * hw + 2 * d)


def _inproj_sample_kernel(x_ref, g_ref, lbl_ref, wa_ref, wb_ref, wc_ref,
                          q_ref, k_ref, v_ref, ga_ref, qi_ref, ki_ref, wi_ref, *tail, layer):
    xn = _normed(x_ref, g_ref)
    w = ATT_WIDTH
    q_ref[...] = (_dot(xn, wa_ref[:, 0:w]) * (HEAD_DIM ** -0.5)).astype(BF16)
    k_ref[...] = _dot(xn, wa_ref[:, w:2 * w])
    v_ref[...] = _dot(xn, wa_ref[:, 2 * w:3 * w])
    ga_ref[...] = _dot(xn, wa_ref[:, 3 * w:4 * w])
    qi_ref[...] = (_dot(xn, wa_ref[:, 4 * w:4 * w + IDX_WIDTH]) * (IDX_DIM ** -0.5)).astype(BF16)
    kiwi = _dot(xn, wb_ref[...])
    ki_ref[...] = kiwi[:, :IDX_DIM]
    wi_ref[...] = kiwi[:, IDX_DIM:IDX_DIM + IDX_HEADS] * (IDX_HEADS ** -0.5)
    _hgrn_and_merge_gates(xn, wc_ref, 0, lbl_ref, layer, *tail)


def _inproj_prompt_kernel(x_ref, g_ref, lbl_ref, wt_ref, wtok_ref, wb_ref,
                          qt_ref, kt_ref, ktok_ref, vt_ref, vtc_ref, ga_ref, qit_ref, kit_ref, kitok_ref, wit_ref,
                          *tail, layer):
    xn = _normed(x_ref, g_ref)
    w = ATT_WIDTH

    def tproj(lo, hi):
        return _dot_nt(wt_ref[lo:hi, :], xn)

    qt_ref[0] = (tproj(0, w) * (HEAD_DIM ** -0.5 * LOG2E)).astype(BF16)
    kt_ref[0] = tproj(w, 2 * w)
    vt = tproj(2 * w, 3 * w)
    vt_ref[0] = vt
    vtc_ref[0, 0] = vt.astype(BF16)
    o = 3 * w
    qit_ref[0] = (tproj(o, o + IDX_WIDTH) * (IDX_DIM ** -0.5)).astype(BF16)
    o += IDX_WIDTH
    kit_ref[0] = tproj(o, o + IDX_DIM)
    wit_ref[0] = tproj(o + IDX_DIM, o + IDX_DIM + IDX_HEADS) * (IDX_HEADS ** -0.5)

    ktok_ref[0, 0] = _dot(xn, wtok_ref[:, 0:w]).astype(BF16)
    kitok_ref[0, 0] = _dot(xn, wb_ref[:, 0:IDX_DIM]).astype(BF16)
    ga_ref[...] = _dot(xn, wtok_ref[:, w:2 * w])
    _hgrn_and_merge_gates(xn, wtok_ref, 2 * w, lbl_ref, layer, *tail)


def _inproj_sample(x2, g_pre, lb_logits, wa, wb, wc, layer, tm):
    m, d = x2.shape
    assert m % tm == 0
    row = lambda n, dt: (pl.BlockSpec((tm, n), lambda i: (i, 0)), jax.ShapeDtypeStruct((m, n), dt))
    full = lambda a: pl.BlockSpec(a.shape, lambda i: (0,) * a.ndim)
    outs = [row(ATT_WIDTH, BF16), row(ATT_WIDTH, F32), row(ATT_WIDTH, F32), row(ATT_WIDTH, F32),
            row(IDX_WIDTH, BF16), row(IDX_DIM, F32), row(IDX_HEADS, F32)] + [row(HG_WIDTH, F32)] * 5 + [row(d, F32)] * 2
    ins = [x2, g_pre, lb_logits, wa, wb, wc]
    return pl.pallas_call(
        functools.partial(_inproj_sample_kernel, layer=layer),
        grid=(m // tm,),
        in_specs=[pl.BlockSpec((tm, d), lambda i: (i, 0))] + [full(a) for a in ins[1:]],
        out_specs=[o[0] for o in outs],
        out_shape=[o[1] for o in outs],
        compiler_params=_cparams(("arbitrary",)),
        name="inproj_sample",
    )(*ins)


def _inproj_prompt(x2, g_pre, lb_logits, wt, wtok, wb, layer, tm, batch):
    m, d = x2.shape
    t = m // batch
    assert m % tm == 0 and t % tm == 0
    nt = t // tm
    row = lambda n, dt: (pl.BlockSpec((tm, n), lambda i: (i, 0)), jax.ShapeDtypeStruct((m, n), dt))
    full = lambda a: pl.BlockSpec(a.shape, lambda i: (0,) * a.ndim)
    fmaj = lambda n, dt: (pl.BlockSpec((1, n, tm), lambda i: (i // nt, 0, i % nt)),
                          jax.ShapeDtypeStruct((batch, n, t), dt))
    fchunk = lambda n: (pl.BlockSpec((1, 1, n, tm), lambda i: (i // nt, i % nt, 0, 0)),
                        jax.ShapeDtypeStruct((batch, nt, n, tm), BF16))
    tchunk = lambda n: (pl.BlockSpec((1, 1, tm, n), lambda i: (i // nt, i % nt, 0, 0)),
                        jax.ShapeDtypeStruct((batch, nt, tm, n), BF16))
    outs = [fmaj(ATT_WIDTH, BF16), fmaj(ATT_WIDTH, F32), tchunk(ATT_WIDTH), fmaj(ATT_WIDTH, F32), fchunk(ATT_WIDTH),
            row(ATT_WIDTH, F32), fmaj(IDX_WIDTH, BF16), fmaj(IDX_DIM, F32), tchunk(IDX_DIM), fmaj(IDX_HEADS, F32)]
    outs += [row(HG_WIDTH, F32)] * 5 + [row(d, F32)] * 2
    ins = [x2, g_pre, lb_logits, wt, wtok, wb]
    return pl.pallas_call(
        functools.partial(_inproj_prompt_kernel, layer=layer),
        grid=(m // tm,),
        in_specs=[pl.BlockSpec((tm, d), lambda i: (i, 0))] + [full(a) for a in ins[1:]],
        out_specs=[o[0] for o in outs],
        out_shape=[o[1] for o in outs],
        compiler_params=_cparams(("arbitrary",)),
        name="inproj_prompt",
    )(*ins)


def _f2key(x):
    b = lax.bitcast_convert_type(x, I32)
    return b ^ ((b >> 31) & INT_MAG)


def _key2f(k):
    return lax.bitcast_convert_type(k ^ ((k >> 31) & INT_MAG), F32)


def _select_topk(count, rewrite, top, n_valid, mx, mn, last_pos, index_steps):
    topf = float(top)

    small = n_valid <= top
    lo0 = _f2key(mn)
    hi0 = _f2key(mx) + 1
    conv0 = jnp.logical_and(jnp.logical_not(small), lo0 + 1 >= hi0)
    done0 = jnp.logical_or(small, conv0)
    thr0 = mn
    ghi0 = jnp.zeros(mn.shape, F32)

    def zero_counts():
        return count(lambda x, kpos: x > 0.0), count(lambda x, kpos: x >= 0.0)

    n_pos, n_nonneg = lax.cond(jnp.sum(1.0 - done0.astype(F32)) > 0.0, zero_counts, lambda: (ghi0, ghi0))
    zero_tie = jnp.logical_and(jnp.logical_not(done0), jnp.logical_and(n_pos < topf, n_nonneg >= topf))
    thr0 = jnp.where(zero_tie, 0.0, thr0)
    ghi0 = jnp.where(zero_tie, n_pos, ghi0)
    tie0 = jnp.logical_or(conv0, zero_tie)
    done0 = jnp.logical_or(done0, zero_tie)
    lo0 = jnp.where(n_nonneg > topf, jnp.maximum(lo0, 0), lo0)
    below = n_nonneg < topf
    hi0 = jnp.where(below, jnp.minimum(hi0, 0), hi0)
    ghi0 = jnp.where(jnp.logical_and(below, hi0 == 0), n_nonneg, ghi0)
    n_live0 = jnp.sum(1.0 - done0.astype(F32))
    state0 = (lo0, hi0, ghi0, thr0, done0.astype(I32), tie0.astype(I32), jnp.int32(0))

    def one_pass(st):
        lo, hi, ghi, thr, done_i, tie_i, it = st
        done = done_i > 0
        lof, hif = _key2f(lo), _key2f(hi)
        midf = 0.5 * lof + 0.5 * hif
        use_f = jnp.logical_and(it < FLOAT_BISECT_STEPS, jnp.logical_and(midf > lof, midf < hif))
        mid = jnp.where(use_f, _f2key(midf), (lo & hi) + ((lo ^ hi) >> 1))
        tmid = _key2f(mid)
        g = count(lambda x, kpos: x >= tmid)
        live = jnp.logical_not(done)
        is_eq = jnp.logical_and(live, g == topf)
        up = jnp.logical_and(live, g > topf)
        down = jnp.logical_and(live, g < topf)
        lo = jnp.where(up, mid, lo)
        hi = jnp.where(down, mid, hi)
        ghi = jnp.where(down, g, ghi)
        thr = jnp.where(is_eq, tmid, thr)
        done = jnp.logical_or(done, is_eq)
        conv = jnp.logical_and(jnp.logical_not(done), lo + 1 >= hi)
        thr = jnp.where(conv, _key2f(lo), thr)
        done = jnp.logical_or(done, conv)
        tie_i = jnp.where(conv, 1, tie_i)
        return (lo, hi, ghi, thr, done.astype(I32), tie_i, it + 1)

    def passes(carry):
        st = carry[0]
        for _ in range(PASSES_PER_TEST):
            st = one_pass(st)
        return st, jnp.sum(1.0 - st[4].astype(F32))

    state = lax.cond(n_live0 > 0.0,
                     lambda: lax.fori_loop(0, UNTESTED_PASSES, lambda _, st: one_pass(st), state0),
                     lambda: state0)
    (_, _, ghi, thr, _, tie_i, _), _ = lax.while_loop(lambda carry: carry[1] > 0.0, passes, (state, n_live0))

    tie = tie_i > 0
    n_tie = jnp.sum(tie_i.astype(F32))

    @pl.when(n_tie > 0.0)
    def _():
        need = topf - ghi
        jlo0 = jnp.full(mn.shape, -1, I32)
        jhi0 = jnp.full(mn.shape, 0, I32) + last_pos

        def jbody(_, carry):
            jlo, jhi = carry
            jm = (jlo + jhi) >> 1
            cnt = count(lambda x, kpos: jnp.logical_and(x == thr, kpos <= jm))
            ok = cnt >= need
            return jnp.where(ok, jlo, jm), jnp.where(ok, jm, jhi)

        _, jcut = lax.fori_loop(0, index_steps, jbody, (jlo0, jhi0))
        rewrite(lambda x, kpos: jnp.where(
            jnp.logical_and(jnp.logical_and(tie, x == thr), kpos > jcut), NEG_INF, x))

    rewrite(lambda x, kpos: jnp.where(x >= thr, 0.0, NEG_INF))


def _index_steps(n_positions):
    return max(1, (n_positions - 1).bit_length()) + 1


def _attn_prompt_kernel(qit_ref, wit_ref, kitok_ref, qt_ref, ktok_ref, vt_ref, o_ref,
                        sc_ref, qis_ref, qs_ref, m_ref, acc_ref, s_ref, d_ref, *, qb, kc, top):
    i = pl.program_id(1)
    nkc = ((i + 1) * qb + kc - 1) // kc

    for h in range(IDX_HEADS):
        qis_ref[:, h * qb:(h + 1) * qb] = qit_ref[0, h * IDX_DIM:(h + 1) * IDX_DIM, :]
    feat = lax.broadcasted_iota(I32, (LANES, qb), 0)
    for hp in range(NPAIR):
        qp = qt_ref[0, hp * LANES:(hp + 1) * LANES, :].astype(F32)
        qs_ref[hp] = jnp.concatenate([jnp.where(feat < HEAD_DIM, qp, 0.0),
                                      jnp.where(feat >= HEAD_DIM, qp, 0.0)], axis=1).astype(BF16)

    wit = wit_ref[0]
    wrows = [wit[h:h + 1, :] for h in range(IDX_HEADS)]
    qpos = i * qb + lax.broadcasted_iota(I32, (1, qb), 1)
    krow = lax.broadcasted_iota(I32, (kc, 1), 0)

    def kpos_of(c):
        return c * kc + krow

    half = IDX_HEADS * qb // 2

    def score_a(c, slot):
        kic = kitok_ref[0, c]
        d_ref[slot, :, 0:half] = _dot(kic, qis_ref[:, 0:half])
        d_ref[slot, :, half:2 * half] = _dot(kic, qis_ref[:, half:2 * half])
        return ()

    def score_b(c, slot, _, carry):
        mxp, mnp = carry
        acc = _tree(lambda a, b: a + b,
                    [wrows[h] * jnp.maximum(d_ref[slot, :, h * qb:(h + 1) * qb], 0.0) for h in range(IDX_HEADS)])
        valid = kpos_of(c) <= qpos
        x = jnp.where(valid, acc, NEG_INF)
        sc_ref[c] = x
        mxp = jnp.maximum(mxp, _fold_rows(x, jnp.maximum))
        mnp = jnp.minimum(mnp, _fold_rows(jnp.where(valid, acc, -NEG_INF), jnp.minimum))
        return mxp, mnp

    mxp, mnp = _two_stage(nkc, score_a, score_b, (jnp.full((SUBLANES, qb), NEG_INF, F32),
                                                  jnp.full((SUBLANES, qb), -NEG_INF, F32)))
    mx = jnp.max(mxp, axis=0, keepdims=True)
    mn = jnp.min(mnp, axis=0, keepdims=True)

    def count(pred):
        def body(c, acc):
            return acc + _fold_rows(jnp.where(pred(sc_ref[c], kpos_of(c)), 1.0, 0.0), lambda a, b: a + b)
        return jnp.sum(_chunk_loop2(nkc, body, jnp.zeros((SUBLANES, qb), F32)), axis=0, keepdims=True)

    def rewrite(fn):
        def body(c, _):
            sc_ref[c] = fn(sc_ref[c], kpos_of(c))
            return 0
        lax.fori_loop(0, nkc, body, 0)

    _select_topk(count, rewrite, top, qpos + 1, mx, mn, nkc * kc - 1, _index_steps(sc_ref.shape[0] * kc))

    m_ref[...] = jnp.full(m_ref.shape, M_INIT, F32)
    acc_ref[...] = jnp.zeros(acc_ref.shape, F32)
    ones = jnp.ones((BF16_ROWS, kc), BF16)

    def att_a(c, slot):
        bias = sc_ref[c]
        bias2 = jnp.concatenate([bias, bias], axis=1)
        cmax = []
        for hp in range(NPAIR):
            s = _dot(ktok_ref[0, c, :, hp * LANES:(hp + 1) * LANES], qs_ref[hp]) + bias2
            s_ref[slot, hp] = s
            cmax.append(jnp.max(_fold_rows(s, jnp.maximum), axis=0, keepdims=True))
        return cmax

    def att_b(c, slot, cmax, carry):
        for hp in range(NPAIR):
            m = m_ref[hp]
            m_new = jnp.maximum(m, cmax[hp])
            p = jnp.exp2(s_ref[slot, hp] - m_new).astype(BF16)
            v1 = jnp.concatenate([vt_ref[0, c, hp * LANES:(hp + 1) * LANES, :], ones], axis=0)
            acc_ref[hp] = jnp.exp2(m - m_new) * acc_ref[hp] + _dot(v1, p)
            m_ref[hp] = m_new
        return carry

    _two_stage(nkc, att_a, att_b, 0)

    for hp in range(NPAIR):
        acc = acc_ref[hp]
        out = acc[0:LANES] / acc[LANES:LANES + 1]
        out = jnp.where(feat < HEAD_DIM, out[:, 0:qb], out[:, qb:2 * qb])
        o_ref[0, :, hp * LANES:(hp + 1) * LANES] = out.T


def _attn_prompt(qit, wit, kitok, qt, ktok, vtc, qb):
    b, _, t = qt.shape
    nc, kc = ktok.shape[1], ktok.shape[2]
    top = min(TOPK_MAX, t // 4)
    assert t % qb == 0 and nc * kc == t and kc % qb == 0 and qb % LANES == 0
    qblk = lambda n: pl.BlockSpec((1, n, qb), lambda bi, i: (bi, 0, i))
    whole = lambda a: pl.BlockSpec((1,) + a.shape[1:], lambda bi, i: (bi, 0, 0, 0), pipeline_mode=pl.Buffered(1))
    return pl.pallas_call(
        functools.partial(_attn_prompt_kernel, qb=qb, kc=kc, top=top),
        grid=(b, t // qb),
        in_specs=[qblk(IDX_WIDTH), qblk(IDX_HEADS), whole(kitok), qblk(ATT_WIDTH), whole(ktok), whole(vtc)],
        out_specs=pl.BlockSpec((1, qb, ATT_WIDTH), lambda bi, i: (bi, i, 0)),
        out_shape=jax.ShapeDtypeStruct((b, t, ATT_WIDTH), F32),
        scratch_shapes=[pltpu.VMEM((nc, kc, qb), F32),
                        pltpu.VMEM((IDX_DIM, IDX_HEADS * qb), BF16),
                        pltpu.VMEM((NPAIR, LANES, 2 * qb), BF16),
                        pltpu.VMEM((NPAIR, 1, 2 * qb), F32),
                        pltpu.VMEM((NPAIR, LANES + BF16_ROWS, 2 * qb), F32),
                        pltpu.VMEM((2, NPAIR, kc, 2 * qb), F32),
                        pltpu.VMEM((2, kc, IDX_HEADS * qb), F32)],
        compiler_params=_cparams(("arbitrary", "arbitrary")),
        name="attn_prompt",
    )(qit, wit, kitok, qt, ktok, vtc)


def _split_pair(qp, lane):
    return jnp.concatenate([jnp.where(lane < HEAD_DIM, qp, 0.0), jnp.where(lane >= HEAD_DIM, qp, 0.0)], axis=0)


def _merge_pair(x, lane, rows):
    return jnp.where(lane < HEAD_DIM, x[0:rows], x[rows:2 * rows])


def _indexer_scores(d, wfull, rows):
    return _tree(lambda a, b: a + b,
                 [wfull[h] * jnp.maximum(d[h * rows:(h + 1) * rows], 0.0) for h in range(IDX_HEADS)])


def _index_sample_kernel(pt_ref, qi_ref, wi_ref, kin_ref, cki_hbm, bias_ref,
                         kibuf_ref, qis_ref, knew_ref, sem, *, nseq, tq, npages, top, unroll):
    b0 = pl.program_id(0) * nseq
    past = npages * PAGE
    nkc = npages + 1

    def page_copy(sq, p):
        return pltpu.make_async_copy(cki_hbm.at[pt_ref[b0 + sq, p]], kibuf_ref.at[sq, p], sem)

    for sq in range(nseq):
        lax.fori_loop(0, npages, lambda p, _, sq=sq: (page_copy(sq, p).start(), 0)[1], 0)

    knew_ref[...] = jnp.zeros(knew_ref.shape, F32)
    wfull = []
    for sq in range(nseq):
        knew_ref[sq, 0:tq, :] = kin_ref[sq]
        qi = qi_ref[sq].astype(F32)
        for h in range(IDX_HEADS):
            qis_ref[sq, h * tq:(h + 1) * tq, :] = qi[:, h * IDX_DIM:(h + 1) * IDX_DIM]
        wi = wi_ref[sq]
        wfull.append([jnp.broadcast_to(wi[:, h:h + 1], (tq, PAGE)) for h in range(IDX_HEADS)])
    qpos_seq = past + lax.broadcasted_iota(I32, (tq, 1), 0)
    qpos = jnp.concatenate([qpos_seq] * nseq, axis=0)
    kpos = (lax.broadcasted_iota(I32, (nkc, 1, PAGE), 0) * PAGE + lax.broadcasted_iota(I32, (nkc, 1, PAGE), 2))
    valid_new = kpos[npages] <= qpos_seq

    for sq in range(nseq):
        lax.fori_loop(0, npages, lambda p, _, sq=sq: (page_copy(sq, p).wait(), 0)[1], 0)

    qis = [qis_ref[sq].astype(BF16) for sq in range(nseq)]
    rows = [slice(sq * tq, (sq + 1) * tq) for sq in range(nseq)]

    def score(cc, carry):
        carry = list(carry)
        for sq in range(nseq):
            mxp, mnp = carry[sq]
            for j in range(unroll):
                c = cc * unroll + j
                acc = _indexer_scores(_dot(qis[sq], kibuf_ref[sq, c].astype(BF16)), wfull[sq], tq)
                bias_ref[0, c, rows[sq], :] = acc
                mxp = jnp.maximum(mxp, acc)
                mnp = jnp.minimum(mnp, acc)
            carry[sq] = (mxp, mnp)
        return tuple(carry)

    init = (jnp.full((tq, PAGE), NEG_INF, F32), jnp.full((tq, PAGE), -NEG_INF, F32))
    parts = lax.fori_loop(0, npages // unroll, score, (init,) * nseq)
    mxs, mns = [], []
    for sq in range(nseq):
        acc = _indexer_scores(_dot_nt(qis[sq], knew_ref[sq].astype(BF16)), wfull[sq], tq)
        bias_ref[0, npages, rows[sq], :] = jnp.where(valid_new, acc, NEG_INF)
        mxs.append(jnp.max(jnp.maximum(parts[sq][0], jnp.where(valid_new, acc, NEG_INF)), axis=-1, keepdims=True))
        mns.append(jnp.min(jnp.minimum(parts[sq][1], jnp.where(valid_new, acc, -NEG_INF)), axis=-1, keepdims=True))
    mx = jnp.concatenate(mxs, axis=0)
    mn = jnp.concatenate(mns, axis=0)

    def count(pred):
        ind = jnp.where(pred(bias_ref[0], kpos), 1.0, 0.0)
        return jnp.sum(jnp.sum(ind, axis=0), axis=-1, keepdims=True)

    def rewrite(fn):
        bias_ref[0] = fn(bias_ref[0], kpos)

    _select_topk(count, rewrite, top, qpos + 1, mx, mn, nkc * PAGE - 1, _index_steps(nkc * PAGE))


def _index_sample(page_table, qi, wi, ki_new, cki_t, nseq):
    b, tq, _ = qi.shape
    npages = page_table.shape[1]
    top = min(TOPK_MAX, (npages * PAGE + tq) // 4)
    nkc = npages + 1
    unroll = 8 if npages % 8 == 0 else 1
    assert tq <= PAGE and b % nseq == 0 and tq % SUBLANES == 0
    blk = lambda n: pl.BlockSpec((nseq, tq, n), lambda bi, pt: (bi, 0, 0))
    grid_spec = pltpu.PrefetchScalarGridSpec(
        num_scalar_prefetch=1,
        grid=(b // nseq,),
        in_specs=[blk(IDX_WIDTH), blk(IDX_HEADS), blk(IDX_DIM), pl.BlockSpec(memory_space=pl.ANY)],
        out_specs=pl.BlockSpec((1, nkc, nseq * tq, PAGE), lambda bi, pt: (bi, 0, 0, 0)),
        scratch_shapes=[pltpu.VMEM((nseq, npages, IDX_DIM, PAGE), F32),
                        pltpu.VMEM((nseq, IDX_HEADS * tq, IDX_DIM), F32),
                        pltpu.VMEM((nseq, PAGE, IDX_DIM), F32),
                        pltpu.SemaphoreType.DMA(())],
    )
    return pl.pallas_call(
        functools.partial(_index_sample_kernel, nseq=nseq, tq=tq, npages=npages, top=top, unroll=unroll),
        grid_spec=grid_spec,
        out_shape=jax.ShapeDtypeStruct((b // nseq, nkc, nseq * tq, PAGE), F32),
        compiler_params=_cparams(("arbitrary",)),
        name="index_sample",
    )(page_table, qi, wi, ki_new, cki_t)


def _attn_sample_kernel(pt_ref, q_ref, kn_ref, vn_ref, bias_ref, ck_hbm, cv_hbm, o_ref,
                        kbuf_ref, vbuf_ref, qs_ref, knew_ref, vnew_ref, m_ref, l_ref, acc_ref, s_ref, sems,
                        *, tq, gp, npages, nb):
    b = pl.program_id(0)
    g = pl.program_id(1)
    ng = npages // gp
    step = b * ng + g
    slot = step % KV_SLOTS

    def group_copies(bb, gg, sl):
        out = []
        for j in range(gp):
            page = pt_ref[bb, gg * gp + j]
            out.append(pltpu.make_async_copy(ck_hbm.at[page], kbuf_ref.at[sl, j], sems.at[0, sl]))
            out.append(pltpu.make_async_copy(cv_hbm.at[page], vbuf_ref.at[sl, j], sems.at[1, sl]))
        return out

    def start_step(st):
        for cp in group_copies(st // ng, st % ng, st % KV_SLOTS):
            cp.start()

    ahead = KV_SLOTS - 1
    assert nb * ng >= ahead

    @pl.when(step == 0)
    def _():
        for st in range(ahead):
            start_step(st)

    @pl.when(step + ahead < nb * ng)
    def _():
        start_step(step + ahead)

    lane = lax.broadcasted_iota(I32, (tq, LANES), 1)

    @pl.when(g == 0)
    def _():
        q = q_ref[0].astype(F32)
        for hp in range(NPAIR):
            qs_ref[hp] = _split_pair(q[:, hp * LANES:(hp + 1) * LANES], lane)
        m_ref[...] = jnp.full(m_ref.shape, M_INIT, F32)
        l_ref[...] = jnp.zeros(l_ref.shape, F32)
        acc_ref[...] = jnp.zeros(acc_ref.shape, F32)

    for cp in group_copies(b, g, slot):
        cp.wait()

    def bias2(c):
        bias = bias_ref[0, c]
        return jnp.concatenate([bias, bias], axis=0)

    def stage_scores(hp, s_list):
        for j, s in enumerate(s_list):
            s_ref[hp, j] = s
        return jnp.max(_tree(jnp.maximum, s_list), axis=-1, keepdims=True)

    def apply_update(hp, n, cmax, pv_of):
        m = m_ref[hp]
        m_new = jnp.maximum(m, cmax)
        alpha = jnp.exp(m - m_new)
        ps = [jnp.exp(s_ref[hp, j] - m_new) for j in range(n)]
        l_ref[hp] = alpha * l_ref[hp] + jnp.sum(_tree(lambda a, b: a + b, ps), axis=-1, keepdims=True)
        acc_ref[hp] = alpha * acc_ref[hp] + _tree(lambda a, b: a + b,
                                                  [pv_of(j, p.astype(BF16)) for j, p in enumerate(ps)])
        m_ref[hp] = m_new

    pairs = [slice(hp * LANES, (hp + 1) * LANES) for hp in range(NPAIR)]
    biases = [bias2(g * gp + j) for j in range(gp)]
    cmax = [stage_scores(hp, [_dot(qs_ref[hp].astype(BF16), kbuf_ref[slot, j, pairs[hp], :].astype(BF16)) + biases[j]
                              for j in range(gp)]) for hp in range(NPAIR)]
    for hp in range(NPAIR):
        apply_update(hp, gp, cmax[hp],
                     lambda j, p, hp=hp: _dot_nt(p, vbuf_ref[slot, j, pairs[hp], :].astype(BF16)))

    @pl.when(g == ng - 1)
    def _():
        knew_ref[...] = jnp.zeros(knew_ref.shape, F32)
        vnew_ref[...] = jnp.zeros(vnew_ref.shape, F32)
        knew_ref[0:tq, :] = kn_ref[0]
        vnew_ref[0:tq, :] = vn_ref[0]
        bnew = bias2(npages)
        for hp in range(NPAIR):
            s = _dot_nt(qs_ref[hp].astype(BF16), knew_ref[:, pairs[hp]].astype(BF16)) + bnew
            apply_update(hp, 1, stage_scores(hp, [s]),
                         lambda j, p, hp=hp: _dot(p, vnew_ref[:, pairs[hp]].astype(BF16)))
            o_ref[0, :, pairs[hp]] = _merge_pair(acc_ref[hp] / l_ref[hp], lane, tq)


def _attn_sample(page_table, q, k_new, v_new, bias, ck_t, cv_t, gp):
    b, tq, _ = q.shape
    npages = page_table.shape[1]
    nseq = bias.shape[2] // tq
    assert npages % gp == 0 and bias.shape[1] == npages + 1 and tq <= PAGE
    blk = lambda n: pl.BlockSpec((1, tq, n), lambda bi, gi, pt: (bi, 0, 0))
    grid_spec = pltpu.PrefetchScalarGridSpec(
        num_scalar_prefetch=1,
        grid=(b, npages // gp),
        in_specs=[blk(ATT_WIDTH), blk(ATT_WIDTH), blk(ATT_WIDTH),
                  pl.BlockSpec((1, npages + 1, tq, PAGE), lambda bi, gi, pt: (bi // nseq, 0, bi % nseq, 0)),
                  pl.BlockSpec(memory_space=pl.ANY), pl.BlockSpec(memory_space=pl.ANY)],
        out_specs=blk(ATT_WIDTH),
        scratch_shapes=[pltpu.VMEM((KV_SLOTS, gp, ATT_WIDTH, PAGE), F32),
                        pltpu.VMEM((KV_SLOTS, gp, ATT_WIDTH, PAGE), F32),
                        pltpu.VMEM((NPAIR, 2 * tq, LANES), F32),
                        pltpu.VMEM((PAGE, ATT_WIDTH), F32),
                        pltpu.VMEM((PAGE, ATT_WIDTH), F32),
                        pltpu.VMEM((NPAIR, 2 * tq, 1), F32),
                        pltpu.VMEM((NPAIR, 2 * tq, 1), F32),
                        pltpu.VMEM((NPAIR, 2 * tq, LANES), F32),
                        pltpu.VMEM((NPAIR, gp, 2 * tq, PAGE), F32),
                        pltpu.SemaphoreType.DMA((2, KV_SLOTS))],
    )
    return pl.pallas_call(
        functools.partial(_attn_sample_kernel, tq=tq, gp=gp, npages=npages, nb=b),
        grid_spec=grid_spec,
        out_shape=jax.ShapeDtypeStruct((b, tq, ATT_WIDTH), F32),
        compiler_params=_cparams(("arbitrary", "arbitrary")),
        name="attn_sample",
    )(page_table, q, k_new, v_new, bias, ck_t, cv_t)


def _cumsum_rows(x):
    n = x.shape[0]
    row = lax.broadcasted_iota(I32, x.shape, 0)
    sh = 1
    while sh < n:
        x = x + jnp.where(row >= sh, pltpu.roll(x, sh, 0), 0.0)
        sh *= 2
    return x


def _hgrn_kernel(hq_ref, lf_ref, hk_ref, hv_ref, hg_ref, s0_ref, ng_ref, o_ref, sout_ref,
                 st_ref, cum_ref, *, tb, c, sb):
    t = pl.program_id(2)
    nt = pl.num_programs(2)

    @pl.when(t == 0)
    def _():
        st_ref[...] = s0_ref[0, 0].T

    row8 = lax.broadcasted_iota(I32, (SUBLANES, 1), 0)
    row_c = lax.broadcasted_iota(I32, (c, 1), 0)
    nsb = c // sb

    for j in range(tb // c):
        rows = slice(j * c, (j + 1) * c)
        lf = lf_ref[0, rows, :]
        q = hq_ref[0, rows, :]
        k = hk_ref[0, rows, :]
        v = hv_ref[0, rows, :]
        cum = _cumsum_rows(lf) * LOG2E
        cum_ref[...] = cum
        last = cum[c - 1:c, :]
        st = st_ref[...]
        o = _dot_nt((q * jnp.exp2(cum)).astype(BF16), st.astype(BF16))

        vb = v.astype(BF16)
        o_parts = []
        a_rows = []
        for i in range(nsb):
            base = i * sb
            groups = [slice(base + r, base + r + SUBLANES) for r in range(0, sb, SUBLANES)]
            od = [jnp.zeros((SUBLANES, HG_DV), F32) for _ in groups]
            for s in range(sb):
                sr = base + s
                row = slice(j * c + sr, j * c + sr + 1)
                cum_s, k_s, v_s = cum_ref[sr:sr + 1, :], hk_ref[0, row, :], hv_ref[0, row, :]
                for gi, rr in enumerate(groups):
                    first = gi * SUBLANES
                    if first + SUBLANES <= s:
                        continue
                    ex = cum[rr] - cum_s
                    if first <= s:
                        ex = jnp.where(row8 >= s - first, ex, NEG_INF)
                    a_col = jnp.sum(jnp.exp2(ex) * (q[rr] * k_s), axis=-1, keepdims=True)
                    od[gi] = od[gi] + a_col * v_s
            o_parts.extend(od)
            if i > 0:
                c0 = cum[base:base + 1, :] - lf[base:base + 1, :] * LOG2E
                qs = (q[base:base + sb] * jnp.exp2(cum[base:base + sb] - c0)).astype(BF16)
                ks = (k * jnp.exp2(jnp.where(row_c < base, c0 - cum, NEG_INF))).astype(BF16)
                a_rows.append(_dot_nt(qs, ks))
            elif nsb > 1:
                a_rows.append(jnp.zeros((sb, c), F32))
        o = o + jnp.concatenate(o_parts, axis=0)
        if nsb > 1:
            o = o + _dot(jnp.concatenate(a_rows, axis=0).astype(BF16), vb)

        kd = (k * jnp.exp2(last - cum)).astype(BF16)
        st_ref[...] = st * jnp.exp2(last) + _dot_tn(vb, kd)

        ms = jnp.mean(o * o, axis=-1, keepdims=True)
        hg = hg_ref[0, rows, :]
        o_ref[0, rows, :] = (((o * lax.rsqrt(ms + RMS_EPS)) * ng_ref[...]) * (hg * _sigmoid(hg))).astype(BF16)

    @pl.when(t == nt - 1)
    def _():
        sout_ref[0, 0] = st_ref[...].T


def _hgrn(hq, lf, hk, hv, hg, s0, norm_g, tb, c):
    b, t, _ = hq.shape
    sb = 16
    assert t % tb == 0 and tb % c == 0 and c % sb == 0
    blk = pl.BlockSpec((1, tb, HG_DK), lambda bi, h, ti: (bi, ti, h))
    sblk = pl.BlockSpec((1, 1, HG_DK, HG_DV), lambda bi, h, ti: (bi, h, 0, 0))
    return pl.pallas_call(
        functools.partial(_hgrn_kernel, tb=tb, c=c, sb=sb),
        grid=(b, HG_HEADS, t // tb),
        in_specs=[blk, blk, blk, blk, blk, sblk, pl.BlockSpec((1, HG_DV), lambda bi, h, ti: (0, 0))],
        out_specs=[blk, sblk],
        out_shape=[jax.ShapeDtypeStruct((b, t, HG_WIDTH), BF16),
                   jax.ShapeDtypeStruct((b, HG_HEADS, HG_DK, HG_DV), F32)],
        scratch_shapes=[pltpu.VMEM((HG_DV, HG_DK), F32), pltpu.VMEM((c, HG_DK), F32)],
        compiler_params=_cparams(("arbitrary", "arbitrary", "arbitrary")),
        name="hgrn",
    )(hq, lf, hk, hv, hg, s0, norm_g)


def _merge_kernel(x_ref, att_ref, ga_ref, hb_ref, ma_ref, mb_ref, p_ref,
                  wa_ref, wb_ref, wo_ref, wpg_ref, wpp_ref, gp_ref, o_ref):
    ga = ga_ref[...]
    ya = _dot((att_ref[...] * (ga * _sigmoid(ga))).astype(BF16), wa_ref[...])
    yb = _dot(hb_ref[...], wb_ref[...])
    mixed = _sigmoid(ma_ref[...]) * ya + _sigmoid(mb_ref[...]) * yb
    u = _dot(mixed.astype(BF16), wo_ref[...])
    ms = jnp.mean(u * u, axis=-1, keepdims=True)
    h1 = x_ref[...] + (u * lax.rsqrt(ms + RMS_EPS)) * gp_ref[...]
    gate = _sigmoid(_dot(h1.astype(BF16), wpg_ref[...]))
    o_ref[...] = h1 + gate * _dot(p_ref[...].astype(BF16), wpp_ref[...])


def _merge(x2, att, ga, hb, ma, mb, p2, w_a, w_b, w_o, w_pg, w_pp, g_post, tm):
    m, d = x2.shape
    assert m % tm == 0
    row = lambda a: pl.BlockSpec((tm, a.shape[1]), lambda i: (i, 0))
    full = lambda a: pl.BlockSpec(a.shape, lambda i: (0, 0))
    acts = (x2, att, ga, hb, ma, mb, p2)
    wts = (w_a, w_b, w_o, w_pg, w_pp, g_post)
    return pl.pallas_call(
        _merge_kernel,
        grid=(m // tm,),
        in_specs=[row(a) for a in acts] + [full(w) for w in wts],
        out_specs=pl.BlockSpec((tm, d), lambda i: (i, 0)),
        out_shape=jax.ShapeDtypeStruct((m, d), F32),
        compiler_params=_cparams(("arbitrary",)),
        name="merge",
    )(*acts, *wts)


def _pick_rows(m, pref):
    return pref if m % pref == 0 else m


def kernel(x_prompt, x_sample, p_prompt, p_sample, cache_k, cache_v, cache_kidx, state_hgrn, page_table,
           g_pre, g_post, w_in, hgrn_lb_logits, hgrn_norm_g, w_branch_a, w_branch_b, w_out,
           w_ple_gate, w_ple_proj):
    depth = w_in.shape[0]
    bp, tp, d = x_prompt.shape
    bs, ts, _ = x_sample.shape
    nphys = cache_k.shape[1]
    hp, hs = x_prompt, x_sample
    outs = [[] for _ in range(8)]
    w = ATT_WIDTH
    a_end = 4 * w + IDX_WIDTH
    b_end = a_end + IDX_DIM + IDX_HEADS

    for i in range(depth):
        wi_ = w_in[i].astype(BF16)
        wa, wb, wc = wi_[:, :a_end], wi_[:, a_end:b_end], wi_[:, b_end:]
        wt = jnp.concatenate([wi_[:, 0:3 * w], wi_[:, 4 * w:b_end]], axis=1).T
        wtok = jnp.concatenate([wi_[:, w:2 * w], wi_[:, 3 * w:4 * w], wc], axis=1)
        g_pre_i = g_pre[i].reshape(1, d)
        wts = (w_branch_a[i].astype(BF16), w_branch_b[i].astype(BF16), w_out[i].astype(BF16),
               w_ple_gate[i].astype(BF16), w_ple_proj[i].astype(BF16), g_post[i].reshape(1, d))
        norm_g = hgrn_norm_g[i].reshape(1, HG_DV)

        mp = bp * tp
        kc = _pick_rows(tp, 256)
        (qt, kt, ktok, vt, vtc, ga, qit, kit, kitok, wit, hq, lf, hk, hv, hg, ma, mb) = _inproj_prompt(
            hp.reshape(mp, d), g_pre_i, hgrn_lb_logits, wt, wtok, wb, i, kc, bp)
        r3 = lambda a, b_=bp, t_=tp: a.reshape(b_, t_, a.shape[-1])
        att = _attn_prompt(qit, wit, kitok, qt, ktok, vtc, qb=_pick_rows(tp, 256))
        s0 = jnp.zeros((bp, HG_HEADS, HG_DK, HG_DV), F32)
        hb, s_p = _hgrn(r3(hq), r3(lf), r3(hk), r3(hv), r3(hg), s0, norm_g, tb=_pick_rows(tp, 512), c=64)
        y = _merge(hp.reshape(mp, d), att.reshape(mp, w), ga, hb.reshape(mp, HG_WIDTH), ma, mb,
                   p_prompt[i].reshape(mp, -1), *wts, _pick_rows(mp, 512))
        hp = y.reshape(bp, tp, d)
        outs[0].append(kt.reshape(bp, N_HEADS, HEAD_DIM, tp).transpose(0, 3, 1, 2))
        outs[1].append(vt.reshape(bp, N_HEADS, HEAD_DIM, tp).transpose(0, 3, 1, 2))
        outs[2].append(kit.transpose(0, 2, 1))
        outs[3].append(s_p.astype(state_hgrn.dtype))

        ms = bs * ts
        (q, k, v, ga, qi, ki, wi, hq, lf, hk, hv, hg, ma, mb) = _inproj_sample(
            hs.reshape(ms, d), g_pre_i, hgrn_lb_logits, wa, wb, wc, i, _pick_rows(ms, 256))
        r3 = lambda a, b_=bs, t_=ts: a.reshape(b_, t_, a.shape[-1])
        cki_t = cache_kidx[i].transpose(0, 2, 1)
        ck_t = cache_k[i].transpose(0, 2, 3, 1).reshape(nphys, w, PAGE)
        cv_t = cache_v[i].transpose(0, 2, 3, 1).reshape(nphys, w, PAGE)
        bias = _index_sample(page_table, r3(qi), r3(wi), r3(ki), cki_t, nseq=next(n for n in (8, 4, 2, 1) if bs % n == 0))
        att = _attn_sample(page_table, r3(q), r3(k), r3(v), bias, ck_t, cv_t, gp=8)
        tpad = -ts % 16
        pad = lambda a: jnp.pad(r3(a), ((0, 0), (0, tpad), (0, 0)))
        hb, s_s = _hgrn(pad(hq), pad(lf), pad(hk), pad(hv), pad(hg), state_hgrn[i], norm_g,
                        tb=ts + tpad, c=ts + tpad)
        y = _merge(hs.reshape(ms, d), att.reshape(ms, w), ga, hb[:, :ts].reshape(ms, HG_WIDTH), ma, mb,
                   p_sample[i].reshape(ms, -1), *wts, _pick_rows(ms, 256))
        hs = y.reshape(bs, ts, d)
        outs[4].append(k.reshape(bs, ts, N_HEADS, HEAD_DIM))
        outs[5].append(v.reshape(bs, ts, N_HEADS, HEAD_DIM))
        outs[6].append(ki.reshape(bs, ts, IDX_DIM))
        outs[7].append(s_s.astype(state_hgrn.dtype))

    st = [jnp.stack(o) for o in outs]
    return (hp, hs, st[0], st[1], st[2], st[3], st[4], st[5], st[6], st[7])
```

```python
import functools

import jax
import jax.numpy as jnp
from jax import lax
from jax.experimental import pallas as pl
from jax.experimental.pallas import tpu as pltpu

F32 = jnp.float32
BF16 = jnp.bfloat16
I32 = jnp.int32

N_HEADS = 8
HEAD_DIM = 64
ATT_WIDTH = N_HEADS * HEAD_DIM
IDX_HEADS = 8
IDX_DIM = 64
IDX_WIDTH = IDX_HEADS * IDX_DIM
TOPK_MAX = 256
HG_HEADS = 4
HG_DK = 128
HG_DV = 128
HG_WIDTH = HG_HEADS * HG_DV
PAGE = 128
RMS_EPS = 1e-6

LANES = 128
SUBLANES = 8
BF16_ROWS = 16
VMEM_LIMIT = 56 * 1024 * 1024
NEG_INF = float("-inf")
M_INIT = -1e30
INT_MAG = 0x7FFFFFFF
NPAIR = N_HEADS // 2
LOG2E = 1.4426950408889634
FLOAT_BISECT_STEPS = 12
UNTESTED_PASSES = 9
PASSES_PER_TEST = 3
KV_SLOTS = 3


def _cparams(sem):
    return pltpu.CompilerParams(dimension_semantics=sem, vmem_limit_bytes=VMEM_LIMIT)


def _sigmoid(x):
    return 1.0 / (1.0 + jnp.exp(-x))


def _dot(a, b):
    return jnp.dot(a, b, preferred_element_type=F32)


def _dot_nt(a, b):
    return lax.dot_general(a, b, (((1,), (1,)), ((), ())), preferred_element_type=F32)


def _dot_tn(a, b):
    return lax.dot_general(a, b, (((0,), (0,)), ((), ())), preferred_element_type=F32)


def _tree(op, xs):
    xs = list(xs)
    while len(xs) > 1:
        xs = [op(xs[i], xs[i + 1]) if i + 1 < len(xs) else xs[i] for i in range(0, len(xs), 2)]
    return xs[0]


def _fold_rows(x, op):
    return _tree(op, [x[r:r + SUBLANES] for r in range(0, x.shape[0], SUBLANES)])


def _two_stage(n, stage_a, stage_b, carry):
    def two(j, st):
        a0, carry = st
        a1 = stage_a(2 * j + 1, 1)
        carry = stage_b(2 * j, 0, a0, carry)
        a0 = stage_a(2 * j + 2, 0)
        carry = stage_b(2 * j + 1, 1, a1, carry)
        return a0, carry

    npairs = (n - 1) // 2
    a0, carry = lax.fori_loop(0, npairs, two, (stage_a(0, 0), carry))
    last = 2 * npairs

    def tail2():
        a1 = stage_a(last + 1, 1)
        return stage_b(last + 1, 1, a1, stage_b(last, 0, a0, carry))

    return lax.cond(last + 1 < n, tail2, lambda: stage_b(last, 0, a0, carry))


def _chunk_loop2(n, body, carry):
    carry = lax.fori_loop(0, n // 2, lambda j, cr: body(2 * j + 1, body(2 * j, cr)), carry)
    return lax.cond(n % 2 == 1, lambda: body(n - 1, carry), lambda: carry)


def _normed(x_ref, g_ref):
    x = x_ref[...]
    var = jnp.mean(x * x, axis=-1, keepdims=True)
    return ((x * lax.rsqrt(var + RMS_EPS)) * g_ref[...]).astype(BF16)


def _hgrn_and_merge_gates(xn, w_ref, off, lbl_ref, layer, hq_ref, lf_ref, hk_ref, hv_ref, hg_ref, ma_ref, mb_ref):
    def proj(lo, hi):
        return _dot(xn, w_ref[:, off + lo:off + hi])

    lg = lbl_ref[...]
    e = jnp.exp(lg - jnp.max(lg, axis=0, keepdims=True))
    sm = e / jnp.sum(e, axis=0, keepdims=True)
    lb = jnp.sum(sm[:layer + 1], axis=0, keepdims=True)
    one_m = 1.0 - lb

    hw = HG_WIDTH
    hq = proj(0, hw)
    hq_ref[...] = hq * _sigmoid(hq)
    zf = proj(hw, 2 * hw)
    lf_ref[...] = jnp.log(lb + one_m * _sigmoid(zf))
    hk_ref[...] = one_m * _sigmoid(-zf)
    hv_ref[...] = proj(2 * hw, 3 * hw)
    hg_ref[...] = proj(3 * hw, 4 * hw)
    d = ma_ref.shape[-1]
    ma_ref[...] = proj(4 * hw, 4 * hw + d)
    mb_ref[...] = proj(4 * hw + d, 4 * hw + 2 * d)


def _inproj_sample_kernel(x_ref, g_ref, lbl_ref, wa_ref, wb_ref, wc_ref,
                          q_ref, k_ref, v_ref, ga_ref, qi_ref, ki_ref, wi_ref, *tail, layer):
    xn = _normed(x_ref, g_ref)
    w = ATT_WIDTH
    q_ref[...] = (_dot(xn, wa_ref[:, 0:w]) * (HEAD_DIM ** -0.5)).astype(BF16)
    k_ref[...] = _dot(xn, wa_ref[:, w:2 * w])
    v_ref[...] = _dot(xn, wa_ref[:, 2 * w:3 * w])
    ga_ref[...] = _dot(xn, wa_ref[:, 3 * w:4 * w])
    qi_ref[...] = (_dot(xn, wa_ref[:, 4 * w:4 * w + IDX_WIDTH]) * (IDX_DIM ** -0.5)).astype(BF16)
    kiwi = _dot(xn, wb_ref[...])
    ki_ref[...] = kiwi[:, :IDX_DIM]
    wi_ref[...] = kiwi[:, IDX_DIM:IDX_DIM + IDX_HEADS] * (IDX_HEADS ** -0.5)
    _hgrn_and_merge_gates(xn, wc_ref, 0, lbl_ref, layer, *tail)


def _inproj_prompt_kernel(x_ref, g_ref, lbl_ref, wt_ref, wtok_ref, wb_ref,
                          qt_ref, kt_ref, ktok_ref, vt_ref, vtc_ref, ga_ref, qit_ref, kit_ref, kitok_ref, wit_ref,
                          *tail, layer):
    xn = _normed(x_ref, g_ref)
    w = ATT_WIDTH

    def tproj(lo, hi):
        return _dot_nt(wt_ref[lo:hi, :], xn)

    qt_ref[0] = (tproj(0, w) * (HEAD_DIM ** -0.5 * LOG2E)).astype(BF16)
    kt_ref[0] = tproj(w, 2 * w)
    vt = tproj(2 * w, 3 * w)
    vt_ref[0] = vt
    vtc_ref[0, 0] = vt.astype(BF16)
    o = 3 * w
    qit_ref[0] = (tproj(o, o + IDX_WIDTH) * (IDX_DIM ** -0.5)).astype(BF16)
    o += IDX_WIDTH
    kit_ref[0] = tproj(o, o + IDX_DIM)
    wit_ref[0] = tproj(o + IDX_DIM, o + IDX_DIM + IDX_HEADS) * (IDX_HEADS ** -0.5)

    ktok_ref[0, 0] = _dot(xn, wtok_ref[:, 0:w]).astype(BF16)
    kitok_ref[0, 0] = _dot(xn, wb_ref[:, 0:IDX_DIM]).astype(BF16)
    ga_ref[...] = _dot(xn, wtok_ref[:, w:2 * w])
    _hgrn_and_merge_gates(xn, wtok_ref, 2 * w, lbl_ref, layer, *tail)


def _inproj_sample(x2, g_pre, lb_logits, wa, wb, wc, layer, tm):
    m, d = x2.shape
    assert m % tm == 0
    row = lambda n, dt: (pl.BlockSpec((tm, n), lambda i: (i, 0)), jax.ShapeDtypeStruct((m, n), dt))
    full = lambda a: pl.BlockSpec(a.shape, lambda i: (0,) * a.ndim)
    outs = [row(ATT_WIDTH, BF16), row(ATT_WIDTH, F32), row(ATT_WIDTH, F32), row(ATT_WIDTH, F32),
            row(IDX_WIDTH, BF16), row(IDX_DIM, F32), row(IDX_HEADS, F32)] + [row(HG_WIDTH, F32)] * 5 + [row(d, F32)] * 2
    ins = [x2, g_pre, lb_logits, wa, wb, wc]
    return pl.pallas_call(
        functools.partial(_inproj_sample_kernel, layer=layer),
        grid=(m // tm,),
        in_specs=[pl.BlockSpec((tm, d), lambda i: (i, 0))] + [full(a) for a in ins[1:]],
        out_specs=[o[0] for o in outs],
        out_shape=[o[1] for o in outs],
        compiler_params=_cparams(("arbitrary",)),
        name="inproj_sample",
    )(*ins)


def _inproj_prompt(x2, g_pre, lb_logits, wt, wtok, wb, layer, tm, batch):
    m, d = x2.shape
    t = m // batch
    assert m % tm == 0 and t % tm == 0
    nt = t // tm
    row = lambda n, dt: (pl.BlockSpec((tm, n), lambda i: (i, 0)), jax.ShapeDtypeStruct((m, n), dt))
    full = lambda a: pl.BlockSpec(a.shape, lambda i: (0,) * a.ndim)
    fmaj = lambda n, dt: (pl.BlockSpec((1, n, tm), lambda i: (i // nt, 0, i % nt)),
                          jax.ShapeDtypeStruct((batch, n, t), dt))
    fchunk = lambda n: (pl.BlockSpec((1, 1, n, tm), lambda i: (i // nt, i % nt, 0, 0)),
                        jax.ShapeDtypeStruct((batch, nt, n, tm), BF16))
    tchunk = lambda n: (pl.BlockSpec((1, 1, tm, n), lambda i: (i // nt, i % nt, 0, 0)),
                        jax.ShapeDtypeStruct((batch, nt, tm, n), BF16))
    outs = [fmaj(ATT_WIDTH, BF16), fmaj(ATT_WIDTH, F32), tchunk(ATT_WIDTH), fmaj(ATT_WIDTH, F32), fchunk(ATT_WIDTH),
            row(ATT_WIDTH, F32), fmaj(IDX_WIDTH, BF16), fmaj(IDX_DIM, F32), tchunk(IDX_DIM), fmaj(IDX_HEADS, F32)]
    outs += [row(HG_WIDTH, F32)] * 5 + [row(d, F32)] * 2
    ins = [x2, g_pre, lb_logits, wt, wtok, wb]
    return pl.pallas_call(
        functools.partial(_inproj_prompt_kernel, layer=layer),
        grid=(m // tm,),
        in_specs=[pl.BlockSpec((tm, d), lambda i: (i, 0))] + [full(a) for a in ins[1:]],
        out_specs=[o[0] for o in outs],
        out_shape=[o[1] for o in outs],
        compiler_params=_cparams(("arbitrary",)),
        name="inproj_prompt",
    )(*ins)


def _f2key(x):
    b = lax.bitcast_convert_type(x, I32)
    return b ^ ((b >> 31) & INT_MAG)


def _key2f(k):
    return lax.bitcast_convert_type(k ^ ((k >> 31) & INT_MAG), F32)


def _select_topk(count, rewrite, top, n_valid, mx, mn, last_pos, index_steps):
    topf = float(top)

    small = n_valid <= top
    lo0 = _f2key(mn)
    hi0 = _f2key(mx) + 1
    conv0 = jnp.logical_and(jnp.logical_not(small), lo0 + 1 >= hi0)
    done0 = jnp.logical_or(small, conv0)
    thr0 = mn
    ghi0 = jnp.zeros(mn.shape, F32)

    def zero_counts():
        return count(lambda x, kpos: x > 0.0), count(lambda x, kpos: x >= 0.0)

    n_pos, n_nonneg = lax.cond(jnp.sum(1.0 - done0.astype(F32)) > 0.0, zero_counts, lambda: (ghi0, ghi0))
    zero_tie = jnp.logical_and(jnp.logical_not(done0), jnp.logical_and(n_pos < topf, n_nonneg >= topf))
    thr0 = jnp.where(zero_tie, 0.0, thr0)
    ghi0 = jnp.where(zero_tie, n_pos, ghi0)
    tie0 = jnp.logical_or(conv0, zero_tie)
    done0 = jnp.logical_or(done0, zero_tie)
    lo0 = jnp.where(n_nonneg > topf, jnp.maximum(lo0, 0), lo0)
    below = n_nonneg < topf
    hi0 = jnp.where(below, jnp.minimum(hi0, 0), hi0)
    ghi0 = jnp.where(jnp.logical_and(below, hi0 == 0), n_nonneg, ghi0)
    n_live0 = jnp.sum(1.0 - done0.astype(F32))
    state0 = (lo0, hi0, ghi0, thr0, done0.astype(I32), tie0.astype(I32), jnp.int32(0))

    def one_pass(st):
        lo, hi, ghi, thr, done_i, tie_i, it = st
        done = done_i > 0
        lof, hif = _key2f(lo), _key2f(hi)
        midf = 0.5 * lof + 0.5 * hif
        use_f = jnp.logical_and(it < FLOAT_BISECT_STEPS, jnp.logical_and(midf > lof, midf < hif))
        mid = jnp.where(use_f, _f2key(midf), (lo & hi) + ((lo ^ hi) >> 1))
        tmid = _key2f(mid)
        g = count(lambda x, kpos: x >= tmid)
        live = jnp.logical_not(done)
        is_eq = jnp.logical_and(live, g == topf)
        up = jnp.logical_and(live, g > topf)
        down = jnp.logical_and(live, g < topf)
        lo = jnp.where(up, mid, lo)
        hi = jnp.where(down, mid, hi)
        ghi = jnp.where(down, g, ghi)
        thr = jnp.where(is_eq, tmid, thr)
        done = jnp.logical_or(done, is_eq)
        conv = jnp.logical_and(jnp.logical_not(done), lo + 1 >= hi)
        thr = jnp.where(conv, _key2f(lo), thr)
        done = jnp.logical_or(done, conv)
        tie_i = jnp.where(conv, 1, tie_i)
        return (lo, hi, ghi, thr, done.astype(I32), tie_i, it + 1)

    def passes(carry):
        st = carry[0]
        for _ in range(PASSES_PER_TEST):
            st = one_pass(st)
        return st, jnp.sum(1.0 - st[4].astype(F32))

    state = lax.cond(n_live0 > 0.0,
                     lambda: lax.fori_loop(0, UNTESTED_PASSES, lambda _, st: one_pass(st), state0),
                     lambda: state0)
    (_, _, ghi, thr, _, tie_i, _), _ = lax.while_loop(lambda carry: carry[1] > 0.0, passes, (state, n_live0))

    tie = tie_i > 0
    n_tie = jnp.sum(tie_i.astype(F32))

    @pl.when(n_tie > 0.0)
    def _():
        need = topf - ghi
        jlo0 = jnp.full(mn.shape, -1, I32)
        jhi0 = jnp.full(mn.shape, 0, I32) + last_pos

        def jbody(_, carry):
            jlo, jhi = carry
            jm = (jlo + jhi) >> 1
            cnt = count(lambda x, kpos: jnp.logical_and(x == thr, kpos <= jm))
            ok = cnt >= need
            return jnp.where(ok, jlo, jm), jnp.where(ok, jm, jhi)

        _, jcut = lax.fori_loop(0, index_steps, jbody, (jlo0, jhi0))
        rewrite(lambda x, kpos: jnp.where(
            jnp.logical_and(jnp.logical_and(tie, x == thr), kpos > jcut), NEG_INF, x))

    rewrite(lambda x, kpos: jnp.where(x >= thr, 0.0, NEG_INF))


def _index_steps(n_positions):
    return max(1, (n_positions - 1).bit_length()) + 1


def _attn_prompt_kernel(qit_ref, wit_ref, kitok_ref, qt_ref, ktok_ref, vt_ref, o_ref,
                        sc_ref, qis_ref, qs_ref, m_ref, acc_ref, s_ref, d_ref, *, qb, kc, top):
    i = pl.program_id(1)
    nkc = ((i + 1) * qb + kc - 1) // kc

    for h in range(IDX_HEADS):
        qis_ref[:, h * qb:(h + 1) * qb] = qit_ref[0, h * IDX_DIM:(h + 1) * IDX_DIM, :]
    feat = lax.broadcasted_iota(I32, (LANES, qb), 0)
    for hp in range(NPAIR):
        qp = qt_ref[0, hp * LANES:(hp + 1) * LANES, :].astype(F32)
        qs_ref[hp] = jnp.concatenate([jnp.where(feat < HEAD_DIM, qp, 0.0),
                                      jnp.where(feat >= HEAD_DIM, qp, 0.0)], axis=1).astype(BF16)

    wit = wit_ref[0]
    wrows = [wit[h:h + 1, :] for h in range(IDX_HEADS)]
    qpos = i * qb + lax.broadcasted_iota(I32, (1, qb), 1)
    krow = lax.broadcasted_iota(I32, (kc, 1), 0)

    def kpos_of(c):
        return c * kc + krow

    half = IDX_HEADS * qb // 2

    def score_a(c, slot):
        kic = kitok_ref[0, c]
        d_ref[slot, :, 0:half] = _dot(kic, qis_ref[:, 0:half])
        d_ref[slot, :, half:2 * half] = _dot(kic, qis_ref[:, half:2 * half])
        return ()

    def score_b(c, slot, _, carry):
        mxp, mnp = carry
        acc = _tree(lambda a, b: a + b,
                    [wrows[h] * jnp.maximum(d_ref[slot, :, h * qb:(h + 1) * qb], 0.0) for h in range(IDX_HEADS)])
        valid = kpos_of(c) <= qpos
        x = jnp.where(valid, acc, NEG_INF)
        sc_ref[c] = x
        mxp = jnp.maximum(mxp, _fold_rows(x, jnp.maximum))
        mnp = jnp.minimum(mnp, _fold_rows(jnp.where(valid, acc, -NEG_INF), jnp.minimum))
        return mxp, mnp

    mxp, mnp = _two_stage(nkc, score_a, score_b, (jnp.full((SUBLANES, qb), NEG_INF, F32),
                                                  jnp.full((SUBLANES, qb), -NEG_INF, F32)))
    mx = jnp.max(mxp, axis=0, keepdims=True)
    mn = jnp.min(mnp, axis=0, keepdims=True)

    def count(pred):
        def body(c, acc):
            hit = pred(sc_ref[c], kpos_of(c))
            accs = [acc, jnp.zeros_like(acc), jnp.zeros_like(acc), jnp.zeros_like(acc)]
            for n, r in enumerate(range(0, kc, SUBLANES)):
                accs[n % 4] = jnp.where(hit[r:r + SUBLANES], accs[n % 4] + 1.0, accs[n % 4])
            return (accs[0] + accs[1]) + (accs[2] + accs[3])
        return jnp.sum(_chunk_loop2(nkc, body, jnp.zeros((SUBLANES, qb), F32)), axis=0, keepdims=True)

    def rewrite(fn):
        def body(c, _):
            sc_ref[c] = fn(sc_ref[c], kpos_of(c))
            return 0
        lax.fori_loop(0, nkc, body, 0)

    _select_topk(count, rewrite, top, qpos + 1, mx, mn, nkc * kc - 1, _index_steps(sc_ref.shape[0] * kc))

    m_ref[...] = jnp.full(m_ref.shape, M_INIT, F32)
    acc_ref[...] = jnp.zeros(acc_ref.shape, F32)
    ones = jnp.ones((BF16_ROWS, kc), BF16)

    def att_a(c, slot):
        bias = sc_ref[c]
        bias2 = jnp.concatenate([bias, bias], axis=1)
        cmax = []
        for hp in range(NPAIR):
            s = _dot(ktok_ref[0, c, :, hp * LANES:(hp + 1) * LANES], qs_ref[hp]) + bias2
            s_ref[slot, hp] = s
            cmax.append(jnp.max(_fold_rows(s, jnp.maximum), axis=0, keepdims=True))
        return cmax

    def att_b(c, slot, cmax, carry):
        for hp in range(NPAIR):
            m = m_ref[hp]
            m_new = jnp.maximum(m, cmax[hp])
            p = jnp.exp2(s_ref[slot, hp] - m_new).astype(BF16)
            v1 = jnp.concatenate([vt_ref[0, c, hp * LANES:(hp + 1) * LANES, :], ones], axis=0)
            acc_ref[hp] = jnp.exp2(m - m_new) * acc_ref[hp] + _dot(v1, p)
            m_ref[hp] = m_new
        return carry

    _two_stage(nkc, att_a, att_b, 0)

    for hp in range(NPAIR):
        acc = acc_ref[hp]
        out = acc[0:LANES] / acc[LANES:LANES + 1]
        out = jnp.where(feat < HEAD_DIM, out[:, 0:qb], out[:, qb:2 * qb])
        o_ref[0, :, hp * LANES:(hp + 1) * LANES] = out.T


def _attn_prompt(qit, wit, kitok, qt, ktok, vtc, qb):
    b, _, t = qt.shape
    nc, kc = ktok.shape[1], ktok.shape[2]
    top = min(TOPK_MAX, t // 4)
    assert t % qb == 0 and nc * kc == t and kc % qb == 0 and qb % LANES == 0
    qblk = lambda n: pl.BlockSpec((1, n, qb), lambda bi, i: (bi, 0, i))
    whole = lambda a: pl.BlockSpec((1,) + a.shape[1:], lambda bi, i: (bi, 0, 0, 0), pipeline_mode=pl.Buffered(1))
    return pl.pallas_call(
        functools.partial(_attn_prompt_kernel, qb=qb, kc=kc, top=top),
        grid=(b, t // qb),
        in_specs=[qblk(IDX_WIDTH), qblk(IDX_HEADS), whole(kitok), qblk(ATT_WIDTH), whole(ktok), whole(vtc)],
        out_specs=pl.BlockSpec((1, qb, ATT_WIDTH), lambda bi, i: (bi, i, 0)),
        out_shape=jax.ShapeDtypeStruct((b, t, ATT_WIDTH), F32),
        scratch_shapes=[pltpu.VMEM((nc, kc, qb), F32),
                        pltpu.VMEM((IDX_DIM, IDX_HEADS * qb), BF16),
                        pltpu.VMEM((NPAIR, LANES, 2 * qb), BF16),
                        pltpu.VMEM((NPAIR, 1, 2 * qb), F32),
                        pltpu.VMEM((NPAIR, LANES + BF16_ROWS, 2 * qb), F32),
                        pltpu.VMEM((2, NPAIR, kc, 2 * qb), F32),
                        pltpu.VMEM((2, kc, IDX_HEADS * qb), F32)],
        compiler_params=_cparams(("arbitrary", "arbitrary")),
        name="attn_prompt",
    )(qit, wit, kitok, qt, ktok, vtc)


def _split_pair(qp, lane):
    return jnp.concatenate([jnp.where(lane < HEAD_DIM, qp, 0.0), jnp.where(lane >= HEAD_DIM, qp, 0.0)], axis=0)


def _merge_pair(x, lane, rows):
    return jnp.where(lane < HEAD_DIM, x[0:rows], x[rows:2 * rows])


def _indexer_scores(d, wfull, rows):
    return _tree(lambda a, b: a + b,
                 [wfull[h] * jnp.maximum(d[h * rows:(h + 1) * rows], 0.0) for h in range(IDX_HEADS)])


def _index_sample_kernel(pt_ref, qi_ref, wi_ref, kin_ref, cki_hbm, bias_ref,
                         kibuf_ref, qis_ref, knew_ref, sem, *, nseq, tq, npages, top, unroll):
    b0 = pl.program_id(0) * nseq
    past = npages * PAGE
    nkc = npages + 1

    def page_copy(sq, p):
        return pltpu.make_async_copy(cki_hbm.at[pt_ref[b0 + sq, p]], kibuf_ref.at[sq, p], sem)

    for sq in range(nseq):
        def start_two(p2, _, sq=sq):
            page_copy(sq, 2 * p2).start(priority=0)
            page_copy(sq, 2 * p2 + 1).start(priority=1)
            return 0
        assert npages % 2 == 0
        lax.fori_loop(0, npages // 2, start_two, 0)

    knew_ref[...] = jnp.zeros(knew_ref.shape, F32)
    wfull = []
    for sq in range(nseq):
        knew_ref[sq, 0:tq, :] = kin_ref[sq]
        qi = qi_ref[sq].astype(F32)
        for h in range(IDX_HEADS):
            qis_ref[sq, h * tq:(h + 1) * tq, :] = qi[:, h * IDX_DIM:(h + 1) * IDX_DIM]
        wi = wi_ref[sq]
        wfull.append([jnp.broadcast_to(wi[:, h:h + 1], (tq, PAGE)) for h in range(IDX_HEADS)])
    qpos_seq = past + lax.broadcasted_iota(I32, (tq, 1), 0)
    qpos = jnp.concatenate([qpos_seq] * nseq, axis=0)
    kpos = (lax.broadcasted_iota(I32, (nkc, 1, PAGE), 0) * PAGE + lax.broadcasted_iota(I32, (nkc, 1, PAGE), 2))
    valid_new = kpos[npages] <= qpos_seq

    for sq in range(nseq):
        lax.fori_loop(0, npages, lambda p, _, sq=sq: (page_copy(sq, p).wait(), 0)[1], 0)

    qis = [qis_ref[sq].astype(BF16) for sq in range(nseq)]
    rows = [slice(sq * tq, (sq + 1) * tq) for sq in range(nseq)]

    def score(cc, carry):
        carry = list(carry)
        for sq in range(nseq):
            mxp, mnp = carry[sq]
            for j in range(unroll):
                c = cc * unroll + j
                acc = _indexer_scores(_dot(qis[sq], kibuf_ref[sq, c].astype(BF16)), wfull[sq], tq)
                bias_ref[0, c, rows[sq], :] = acc
                mxp = jnp.maximum(mxp, acc)
                mnp = jnp.minimum(mnp, acc)
            carry[sq] = (mxp, mnp)
        return tuple(carry)

    init = (jnp.full((tq, PAGE), NEG_INF, F32), jnp.full((tq, PAGE), -NEG_INF, F32))
    parts = lax.fori_loop(0, npages // unroll, score, (init,) * nseq)
    mxs, mns = [], []
    for sq in range(nseq):
        acc = _indexer_scores(_dot_nt(qis[sq], knew_ref[sq].astype(BF16)), wfull[sq], tq)
        bias_ref[0, npages, rows[sq], :] = jnp.where(valid_new, acc, NEG_INF)
        mxs.append(jnp.max(jnp.maximum(parts[sq][0], jnp.where(valid_new, acc, NEG_INF)), axis=-1, keepdims=True))
        mns.append(jnp.min(jnp.minimum(parts[sq][1], jnp.where(valid_new, acc, -NEG_INF)), axis=-1, keepdims=True))
    mx = jnp.concatenate(mxs, axis=0)
    mn = jnp.concatenate(mns, axis=0)

    def count(pred):
        ind = jnp.where(pred(bias_ref[0], kpos), 1.0, 0.0)
        return jnp.sum(jnp.sum(ind, axis=0), axis=-1, keepdims=True)

    def rewrite(fn):
        bias_ref[0] = fn(bias_ref[0], kpos)

    _select_topk(count, rewrite, top, qpos + 1, mx, mn, nkc * PAGE - 1, _index_steps(nkc * PAGE))


def _index_sample(page_table, qi, wi, ki_new, cki_t, nseq):
    b, tq, _ = qi.shape
    npages = page_table.shape[1]
    top = min(TOPK_MAX, (npages * PAGE + tq) // 4)
    nkc = npages + 1
    unroll = 8 if npages % 8 == 0 else 1
    assert tq <= PAGE and b % nseq == 0 and tq % SUBLANES == 0
    blk = lambda n: pl.BlockSpec((nseq, tq, n), lambda bi, pt: (bi, 0, 0))
    grid_spec = pltpu.PrefetchScalarGridSpec(
        num_scalar_prefetch=1,
        grid=(b // nseq,),
        in_specs=[blk(IDX_WIDTH), blk(IDX_HEADS), blk(IDX_DIM), pl.BlockSpec(memory_space=pl.ANY)],
        out_specs=pl.BlockSpec((1, nkc, nseq * tq, PAGE), lambda bi, pt: (bi, 0, 0, 0)),
        scratch_shapes=[pltpu.VMEM((nseq, npages, IDX_DIM, PAGE), F32),
                        pltpu.VMEM((nseq, IDX_HEADS * tq, IDX_DIM), F32),
                        pltpu.VMEM((nseq, PAGE, IDX_DIM), F32),
                        pltpu.SemaphoreType.DMA(())],
    )
    return pl.pallas_call(
        functools.partial(_index_sample_kernel, nseq=nseq, tq=tq, npages=npages, top=top, unroll=unroll),
        grid_spec=grid_spec,
        out_shape=jax.ShapeDtypeStruct((b // nseq, nkc, nseq * tq, PAGE), F32),
        compiler_params=_cparams(("arbitrary",)),
        name="index_sample",
    )(page_table, qi, wi, ki_new, cki_t)


def _attn_sample_kernel(pt_ref, q_ref, kn_ref, vn_ref, bias_ref, ck_hbm, cv_hbm, o_ref,
                        kbuf_ref, vbuf_ref, qs_ref, knew_ref, vnew_ref, m_ref, l_ref, acc_ref, s_ref, sems,
                        *, tq, gp, npages, nb):
    b = pl.program_id(0)
    g = pl.program_id(1)
    ng = npages // gp
    step = b * ng + g
    slot = step % KV_SLOTS

    def group_copies(bb, gg, sl):
        out = []
        for j in range(gp):
            page = pt_ref[bb, gg * gp + j]
            out.append(pltpu.make_async_copy(ck_hbm.at[page], kbuf_ref.at[sl, j], sems.at[0, sl]))
            out.append(pltpu.make_async_copy(cv_hbm.at[page], vbuf_ref.at[sl, j], sems.at[1, sl]))
        return out

    def start_step(st):
        for cp in group_copies(st // ng, st % ng, st % KV_SLOTS):
            cp.start()

    ahead = KV_SLOTS - 1
    assert nb * ng >= ahead

    @pl.when(step == 0)
    def _():
        for st in range(ahead):
            start_step(st)

    @pl.when(step + ahead < nb * ng)
    def _():
        start_step(step + ahead)

    lane = lax.broadcasted_iota(I32, (tq, LANES), 1)

    @pl.when(g == 0)
    def _():
        q = q_ref[0].astype(F32)
        for hp in range(NPAIR):
            qs_ref[hp] = _split_pair(q[:, hp * LANES:(hp + 1) * LANES], lane)
        m_ref[...] = jnp.full(m_ref.shape, M_INIT, F32)
        l_ref[...] = jnp.zeros(l_ref.shape, F32)
        acc_ref[...] = jnp.zeros(acc_ref.shape, F32)

    for cp in group_copies(b, g, slot):
        cp.wait()

    def bias2(c):
        bias = bias_ref[0, c]
        return jnp.concatenate([bias, bias], axis=0)

    def stage_scores(hp, s_list):
        for j, s in enumerate(s_list):
            s_ref[hp, j] = s
        return jnp.max(_tree(jnp.maximum, s_list), axis=-1, keepdims=True)

    def apply_update(hp, n, cmax, pv_of):
        m = m_ref[hp]
        m_new = jnp.maximum(m, cmax)
        alpha = jnp.exp(m - m_new)
        ps = [jnp.exp(s_ref[hp, j] - m_new) for j in range(n)]
        l_ref[hp] = alpha * l_ref[hp] + jnp.sum(_tree(lambda a, b: a + b, ps), axis=-1, keepdims=True)
        acc_ref[hp] = alpha * acc_ref[hp] + _tree(lambda a, b: a + b,
                                                  [pv_of(j, p.astype(BF16)) for j, p in enumerate(ps)])
        m_ref[hp] = m_new

    pairs = [slice(hp * LANES, (hp + 1) * LANES) for hp in range(NPAIR)]
    biases = [bias2(g * gp + j) for j in range(gp)]
    cmax = [stage_scores(hp, [_dot(qs_ref[hp].astype(BF16), kbuf_ref[slot, j, pairs[hp], :].astype(BF16)) + biases[j]
                              for j in range(gp)]) for hp in range(NPAIR)]
    for hp in range(NPAIR):
        apply_update(hp, gp, cmax[hp],
                     lambda j, p, hp=hp: _dot_nt(p, vbuf_ref[slot, j, pairs[hp], :].astype(BF16)))

    @pl.when(g == ng - 1)
    def _():
        knew_ref[...] = jnp.zeros(knew_ref.shape, F32)
        vnew_ref[...] = jnp.zeros(vnew_ref.shape, F32)
        knew_ref[0:tq, :] = kn_ref[0]
        vnew_ref[0:tq, :] = vn_ref[0]
        bnew = bias2(npages)
        for hp in range(NPAIR):
            s = _dot_nt(qs_ref[hp].astype(BF16), knew_ref[:, pairs[hp]].astype(BF16)) + bnew
            apply_update(hp, 1, stage_scores(hp, [s]),
                         lambda j, p, hp=hp: _dot(p, vnew_ref[:, pairs[hp]].astype(BF16)))
            o_ref[0, :, pairs[hp]] = _merge_pair(acc_ref[hp] / l_ref[hp], lane, tq)


def _attn_sample(page_table, q, k_new, v_new, bias, ck_t, cv_t, gp):
    b, tq, _ = q.shape
    npages = page_table.shape[1]
    nseq = bias.shape[2] // tq
    assert npages % gp == 0 and bias.shape[1] == npages + 1 and tq <= PAGE
    blk = lambda n: pl.BlockSpec((1, tq, n), lambda bi, gi, pt: (bi, 0, 0))
    grid_spec = pltpu.PrefetchScalarGridSpec(
        num_scalar_prefetch=1,
        grid=(b, npages // gp),
        in_specs=[blk(ATT_WIDTH), blk(ATT_WIDTH), blk(ATT_WIDTH),
                  pl.BlockSpec((1, npages + 1, tq, PAGE), lambda bi, gi, pt: (bi // nseq, 0, bi % nseq, 0)),
                  pl.BlockSpec(memory_space=pl.ANY), pl.BlockSpec(memory_space=pl.ANY)],
        out_specs=blk(ATT_WIDTH),
        scratch_shapes=[pltpu.VMEM((KV_SLOTS, gp, ATT_WIDTH, PAGE), F32),
                        pltpu.VMEM((KV_SLOTS, gp, ATT_WIDTH, PAGE), F32),
                        pltpu.VMEM((NPAIR, 2 * tq, LANES), F32),
                        pltpu.VMEM((PAGE, ATT_WIDTH), F32),
                        pltpu.VMEM((PAGE, ATT_WIDTH), F32),
                        pltpu.VMEM((NPAIR, 2 * tq, 1), F32),
                        pltpu.VMEM((NPAIR, 2 * tq, 1), F32),
                        pltpu.VMEM((NPAIR, 2 * tq, LANES), F32),
                        pltpu.VMEM((NPAIR, gp, 2 * tq, PAGE), F32),
                        pltpu.SemaphoreType.DMA((2, KV_SLOTS))],
    )
    return pl.pallas_call(
        functools.partial(_attn_sample_kernel, tq=tq, gp=gp, npages=npages, nb=b),
        grid_spec=grid_spec,
        out_shape=jax.ShapeDtypeStruct((b, tq, ATT_WIDTH), F32),
        compiler_params=_cparams(("arbitrary", "arbitrary")),
        name="attn_sample",
    )(page_table, q, k_new, v_new, bias, ck_t, cv_t)


def _cumsum_rows(x):
    n = x.shape[0]
    row = lax.broadcasted_iota(I32, x.shape, 0)
    sh = 1
    while sh < n:
        x = x + jnp.where(row >= sh, pltpu.roll(x, sh, 0), 0.0)
        sh *= 2
    return x


def _hgrn_kernel(hq_ref, lf_ref, hk_ref, hv_ref, hg_ref, s0_ref, ng_ref, o_ref, sout_ref,
                 st_ref, cum_ref, *, tb, c, sb):
    t = pl.program_id(2)
    nt = pl.num_programs(2)

    @pl.when(t == 0)
    def _():
        st_ref[...] = s0_ref[0, 0].T

    row8 = lax.broadcasted_iota(I32, (SUBLANES, 1), 0)
    row_c = lax.broadcasted_iota(I32, (c, 1), 0)
    nsb = c // sb

    for j in range(tb // c):
        rows = slice(j * c, (j + 1) * c)
        lf = lf_ref[0, rows, :]
        q = hq_ref[0, rows, :]
        k = hk_ref[0, rows, :]
        v = hv_ref[0, rows, :]
        cum = _cumsum_rows(lf) * LOG2E
        cum_ref[...] = cum
        last = cum[c - 1:c, :]
        st = st_ref[...]
        o = _dot_nt((q * jnp.exp2(cum)).astype(BF16), st.astype(BF16))

        vb = v.astype(BF16)
        o_parts = []
        a_rows = []
        for i in range(nsb):
            base = i * sb
            groups = [slice(base + r, base + r + SUBLANES) for r in range(0, sb, SUBLANES)]
            od = [jnp.zeros((SUBLANES, HG_DV), F32) for _ in groups]
            for s in range(sb):
                sr = base + s
                row = slice(j * c + sr, j * c + sr + 1)
                cum_s, k_s, v_s = cum_ref[sr:sr + 1, :], hk_ref[0, row, :], hv_ref[0, row, :]
                for gi, rr in enumerate(groups):
                    first = gi * SUBLANES
                    if first + SUBLANES <= s:
                        continue
                    ex = cum[rr] - cum_s
                    if first <= s:
                        ex = jnp.where(row8 >= s - first, ex, NEG_INF)
                    a_col = jnp.sum(jnp.exp2(ex) * (q[rr] * k_s), axis=-1, keepdims=True)
                    od[gi] = od[gi] + a_col * v_s
            o_parts.extend(od)
            if i > 0:
                c0 = cum[base:base + 1, :] - lf[base:base + 1, :] * LOG2E
                qs = (q[base:base + sb] * jnp.exp2(cum[base:base + sb] - c0)).astype(BF16)
                ks = (k * jnp.exp2(jnp.where(row_c < base, c0 - cum, NEG_INF))).astype(BF16)
                a_rows.append(_dot_nt(qs, ks))
            elif nsb > 1:
                a_rows.append(jnp.zeros((sb, c), F32))
        o = o + jnp.concatenate(o_parts, axis=0)
        if nsb > 1:
            o = o + _dot(jnp.concatenate(a_rows, axis=0).astype(BF16), vb)

        kd = (k * jnp.exp2(last - cum)).astype(BF16)
        st_ref[...] = st * jnp.exp2(last) + _dot_tn(vb, kd)

        ms = jnp.mean(o * o, axis=-1, keepdims=True)
        hg = hg_ref[0, rows, :]
        o_ref[0, rows, :] = (((o * lax.rsqrt(ms + RMS_EPS)) * ng_ref[...]) * (hg * _sigmoid(hg))).astype(BF16)

    @pl.when(t == nt - 1)
    def _():
        sout_ref[0, 0] = st_ref[...].T


def _hgrn(hq, lf, hk, hv, hg, s0, norm_g, tb, c):
    b, t, _ = hq.shape
    sb = 16
    assert t % tb == 0 and tb % c == 0 and c % sb == 0
    blk = pl.BlockSpec((1, tb, HG_DK), lambda bi, h, ti: (bi, ti, h))
    sblk = pl.BlockSpec((1, 1, HG_DK, HG_DV), lambda bi, h, ti: (bi, h, 0, 0))
    return pl.pallas_call(
        functools.partial(_hgrn_kernel, tb=tb, c=c, sb=sb),
        grid=(b, HG_HEADS, t // tb),
        in_specs=[blk, blk, blk, blk, blk, sblk, pl.BlockSpec((1, HG_DV), lambda bi, h, ti: (0, 0))],
        out_specs=[blk, sblk],
        out_shape=[jax.ShapeDtypeStruct((b, t, HG_WIDTH), BF16),
                   jax.ShapeDtypeStruct((b, HG_HEADS, HG_DK, HG_DV), F32)],
        scratch_shapes=[pltpu.VMEM((HG_DV, HG_DK), F32), pltpu.VMEM((c, HG_DK), F32)],
        compiler_params=_cparams(("arbitrary", "arbitrary", "arbitrary")),
        name="hgrn",
    )(hq, lf, hk, hv, hg, s0, norm_g)


def _merge_kernel(x_ref, att_ref, ga_ref, hb_ref, ma_ref, mb_ref, p_ref,
                  wa_ref, wb_ref, wo_ref, wpg_ref, wpp_ref, gp_ref, o_ref):
    ga = ga_ref[...]
    ya = _dot((att_ref[...] * (ga * _sigmoid(ga))).astype(BF16), wa_ref[...])
    yb = _dot(hb_ref[...], wb_ref[...])
    mixed = _sigmoid(ma_ref[...]) * ya + _sigmoid(mb_ref[...]) * yb
    u = _dot(mixed.astype(BF16), wo_ref[...])
    ms = jnp.mean(u * u, axis=-1, keepdims=True)
    h1 = x_ref[...] + (u * lax.rsqrt(ms + RMS_EPS)) * gp_ref[...]
    gate = _sigmoid(_dot(h1.astype(BF16), wpg_ref[...]))
    o_ref[...] = h1 + gate * _dot(p_ref[...].astype(BF16), wpp_ref[...])


def _merge(x2, att, ga, hb, ma, mb, p2, w_a, w_b, w_o, w_pg, w_pp, g_post, tm):
    m, d = x2.shape
    assert m % tm == 0
    row = lambda a: pl.BlockSpec((tm, a.shape[1]), lambda i: (i, 0))
    full = lambda a: pl.BlockSpec(a.shape, lambda i: (0, 0))
    acts = (x2, att, ga, hb, ma, mb, p2)
    wts = (w_a, w_b, w_o, w_pg, w_pp, g_post)
    return pl.pallas_call(
        _merge_kernel,
        grid=(m // tm,),
        in_specs=[row(a) for a in acts] + [full(w) for w in wts],
        out_specs=pl.BlockSpec((tm, d), lambda i: (i, 0)),
        out_shape=jax.ShapeDtypeStruct((m, d), F32),
        compiler_params=_cparams(("arbitrary",)),
        name="merge",
    )(*acts, *wts)


def _pick_rows(m, pref):
    return pref if m % pref == 0 else m


def kernel(x_prompt, x_sample, p_prompt, p_sample, cache_k, cache_v, cache_kidx, state_hgrn, page_table,
           g_pre, g_post, w_in, hgrn_lb_logits, hgrn_norm_g, w_branch_a, w_branch_b, w_out,
           w_ple_gate, w_ple_proj):
    depth = w_in.shape[0]
    bp, tp, d = x_prompt.shape
    bs, ts, _ = x_sample.shape
    nphys = cache_k.shape[1]
    hp, hs = x_prompt, x_sample
    outs = [[] for _ in range(8)]
    w = ATT_WIDTH
    a_end = 4 * w + IDX_WIDTH
    b_end = a_end + IDX_DIM + IDX_HEADS

    for i in range(depth):
        wi_ = w_in[i].astype(BF16)
        wa, wb, wc = wi_[:, :a_end], wi_[:, a_end:b_end], wi_[:, b_end:]
        wt = jnp.concatenate([wi_[:, 0:3 * w], wi_[:, 4 * w:b_end]], axis=1).T
        wtok = jnp.concatenate([wi_[:, w:2 * w], wi_[:, 3 * w:4 * w], wc], axis=1)
        g_pre_i = g_pre[i].reshape(1, d)
        wts = (w_branch_a[i].astype(BF16), w_branch_b[i].astype(BF16), w_out[i].astype(BF16),
               w_ple_gate[i].astype(BF16), w_ple_proj[i].astype(BF16), g_post[i].reshape(1, d))
        norm_g = hgrn_norm_g[i].reshape(1, HG_DV)

        mp = bp * tp
        kc = _pick_rows(tp, 256)
        (qt, kt, ktok, vt, vtc, ga, qit, kit, kitok, wit, hq, lf, hk, hv, hg, ma, mb) = _inproj_prompt(
            hp.reshape(mp, d), g_pre_i, hgrn_lb_logits, wt, wtok, wb, i, kc, bp)
        r3 = lambda a, b_=bp, t_=tp: a.reshape(b_, t_, a.shape[-1])
        att = _attn_prompt(qit, wit, kitok, qt, ktok, vtc, qb=_pick_rows(tp, 256))
        s0 = jnp.zeros((bp, HG_HEADS, HG_DK, HG_DV), F32)
        hb, s_p = _hgrn(r3(hq), r3(lf), r3(hk), r3(hv), r3(hg), s0, norm_g, tb=_pick_rows(tp, 512), c=64)
        y = _merge(hp.reshape(mp, d), att.reshape(mp, w), ga, hb.reshape(mp, HG_WIDTH), ma, mb,
                   p_prompt[i].reshape(mp, -1), *wts, _pick_rows(mp, 512))
        hp = y.reshape(bp, tp, d)
        outs[0].append(kt.reshape(bp, N_HEADS, HEAD_DIM, tp).transpose(0, 3, 1, 2))
        outs[1].append(vt.reshape(bp, N_HEADS, HEAD_DIM, tp).transpose(0, 3, 1, 2))
        outs[2].append(kit.transpose(0, 2, 1))
        outs[3].append(s_p.astype(state_hgrn.dtype))

        ms = bs * ts
        (q, k, v, ga, qi, ki, wi, hq, lf, hk, hv, hg, ma, mb) = _inproj_sample(
            hs.reshape(ms, d), g_pre_i, hgrn_lb_logits, wa, wb, wc, i, _pick_rows(ms, 256))
        r3 = lambda a, b_=bs, t_=ts: a.reshape(b_, t_, a.shape[-1])
        cki_t = cache_kidx[i].transpose(0, 2, 1)
        ck_t = cache_k[i].transpose(0, 2, 3, 1).reshape(nphys, w, PAGE)
        cv_t = cache_v[i].transpose(0, 2, 3, 1).reshape(nphys, w, PAGE)
        bias = _index_sample(page_table, r3(qi), r3(wi), r3(ki), cki_t, nseq=next(n for n in (8, 4, 2, 1) if bs % n == 0))
        att = _attn_sample(page_table, r3(q), r3(k), r3(v), bias, ck_t, cv_t, gp=8)
        tpad = -ts % 16
        pad = lambda a: jnp.pad(r3(a), ((0, 0), (0, tpad), (0, 0)))
        hb, s_s = _hgrn(pad(hq), pad(lf), pad(hk), pad(hv), pad(hg), state_hgrn[i], norm_g,
                        tb=ts + tpad, c=ts + tpad)
        y = _merge(hs.reshape(ms, d), att.reshape(ms, w), ga, hb[:, :ts].reshape(ms, HG_WIDTH), ma, mb,
                   p_sample[i].reshape(ms, -1), *wts, _pick_rows(ms, 256))
        hs = y.reshape(bs, ts, d)
        outs[4].append(k.reshape(bs, ts, N_HEADS, HEAD_DIM))
        outs[5].append(v.reshape(bs, ts, N_HEADS, HEAD_DIM))
        outs[6].append(ki.reshape(bs, ts, IDX_DIM))
        outs[7].append(s_s.astype(state_hgrn.dtype))

    st = [jnp.stack(o) for o in outs]
    return (hp, hs, st[0], st[1], st[2], st[3], st[4], st[5], st[6], st[7])
```
